```python
import math
import jax, jax.numpy as jnp
from jax import lax
import numpy as np

D_MODEL = 1024
BATCH = 4
SEQ = 4096
DEPTH = 2

N_ATT_HEADS = 8
ATT_HEAD_DIM = 128
ATT_WIDTH = N_ATT_HEADS * ATT_HEAD_DIM
KV_RANK = 256
IDX_HEADS = 8
IDX_DIM = 64
TOPK_MAX = 256
Q_BLOCK = 128
SSM_EXPAND = 2
D_INNER = SSM_EXPAND * D_MODEL
SSM_HEAD_DIM = 64
N_SSM_HEADS = D_INNER // SSM_HEAD_DIM
N_SSM_GROUPS = 4
HEADS_PER_GROUP = N_SSM_HEADS // N_SSM_GROUPS
D_STATE = 128
CONV_WIDTH = 4
CONV_DIM = D_INNER + 2 * N_SSM_GROUPS * D_STATE
CHUNK = 128
D_FF = 4 * D_MODEL
N_MOD = 6
EPS = 1e-6

IN_SPLITS = (ATT_WIDTH, KV_RANK, IDX_HEADS * IDX_DIM, IDX_DIM, IDX_HEADS,
             D_INNER, CONV_DIM, N_SSM_HEADS, D_MODEL, D_MODEL)
D_IN_PROJ = ATT_WIDTH + KV_RANK + IDX_HEADS * IDX_DIM + IDX_DIM + IDX_HEADS + D_INNER + CONV_DIM + N_SSM_HEADS + 2 * D_MODEL

kernel_name = "hybrid_dsa_ssd_adaln_block"


def _normalize(x):
    xf = x.astype(jnp.float32)
    return xf * lax.rsqrt(jnp.mean(xf * xf, axis=-1, keepdims=True) + EPS)


def rms_norm(x, g):
    return (_normalize(x) * g.astype(jnp.float32)).astype(x.dtype)


def _split_last(t, sizes):
    offs = np.cumsum(np.array(sizes))[:-1].tolist()
    return jnp.split(t, offs, axis=-1)


def dsa_attention(q, c_kv, q_idx, k_idx, w_idx, w_uk, w_uv):
    b, s = q.shape[0], q.shape[1]
    topk = min(TOPK_MAX, s // 4)
    n_blocks = s // Q_BLOCK
    scale = ATT_HEAD_DIM ** -0.5
    q_abs = jnp.einsum('bshd,rhd->bshr', q, w_uk)
    key_pos = jnp.arange(s)
    gather_rows = jax.vmap(lambda table, idx: table[idx])

    def one_block(start):
        qa = lax.dynamic_slice_in_dim(q_abs, start, Q_BLOCK, axis=1)
        qi = lax.dynamic_slice_in_dim(q_idx, start, Q_BLOCK, axis=1)
        wi = lax.dynamic_slice_in_dim(w_idx, start, Q_BLOCK, axis=1)
        q_pos = start + jnp.arange(Q_BLOCK)
        causal = key_pos[None, :] <= q_pos[:, None]
        logits = jnp.einsum('bqhd,bsd->bqhs', qi, k_idx)
        index_score = jnp.einsum('bqh,bqhs->bqs', wi.astype(jnp.float32),
                                 jax.nn.relu(logits).astype(jnp.float32))
        index_score = jnp.where(causal[None], index_score, -jnp.inf)
        _, sel = lax.top_k(index_score, topk)
        valid = sel <= q_pos[None, :, None]
        c_sel = gather_rows(c_kv, sel)
        scores = jnp.einsum('bqhr,bqkr->bqhk', qa, c_sel).astype(jnp.float32) * scale
        scores = jnp.where(valid[:, :, None, :], scores, -jnp.inf)
        p = jax.nn.softmax(scores, axis=-1).astype(c_sel.dtype)
        return jnp.einsum('bqhk,bqkr->bqhr', p, c_sel)

    o_lat = lax.map(one_block, jnp.arange(n_blocks) * Q_BLOCK)
    o_lat = jnp.moveaxis(o_lat, 0, 1).reshape(b, s, N_ATT_HEADS, KV_RANK)
    o = jnp.einsum('bshr,rhd->bshd', o_lat, w_uv)
    return o.reshape(b, s, ATT_WIDTH)


def causal_depthwise_conv(x, w, bias):
    y = lax.conv_general_dilated(x, w[:, None, :].astype(x.dtype), window_strides=(1,),
                                 padding=[(CONV_WIDTH - 1, 0)],
                                 dimension_numbers=('NWC', 'WIO', 'NWC'),
                                 feature_group_count=x.shape[-1])
    return y + bias


def ssd_chunked(xs, dt, a, bm, cm):
    b, s, g, hg, p = xs.shape
    n = bm.shape[-1]
    nc = s // CHUNK
    f32 = jnp.float32
    xs = xs.astype(f32).reshape(b, nc, CHUNK, g, hg, p)
    dt = dt.astype(f32).reshape(b, nc, CHUNK, g, hg)
    bm = bm.astype(f32).reshape(b, nc, CHUNK, g, n)
    cm = cm.astype(f32).reshape(b, nc, CHUNK, g, n)
    a_cum = jnp.cumsum(dt * a.astype(f32), axis=2)
    xdt = xs * dt[..., None]
    a_t = jnp.moveaxis(a_cum, 2, -1)
    seg = a_t[..., :, None] - a_t[..., None, :]
    tril = jnp.tril(jnp.ones((CHUNK, CHUNK), dtype=bool))
    decay = jnp.exp(jnp.where(tril, seg, -jnp.inf))
    cb = jnp.einsum('bclgn,bcsgn->bcgls', cm, bm)
    y_diag = jnp.einsum('bcgls,bcghls,bcsghp->bclghp', cb, decay, xdt)
    decay_to_end = jnp.exp(a_cum[:, :, -1:] - a_cum)
    chunk_states = jnp.einsum('bclgn,bclgh,bclghp->bcghpn', bm, decay_to_end, xdt)
    chunk_decay = jnp.exp(a_cum[:, :, -1])

    def carry_step(state, inp):
        st, dec = inp
        return state * dec[..., None, None] + st, state

    init = jnp.zeros((b, g, hg, p, n), f32)
    _, prev = lax.scan(carry_step, init,
                       (jnp.moveaxis(chunk_states, 1, 0), jnp.moveaxis(chunk_decay, 1, 0)))
    prev = jnp.moveaxis(prev, 0, 1)
    y_off = jnp.einsum('bclgn,bcghpn,bclgh->bclghp', cm, prev, jnp.exp(a_cum))
    return (y_diag + y_off).reshape(b, s, g, hg, p)


def hybrid_mixer(h, w_in, kv_norm_g, kidx_norm_g, w_uk, w_uv, conv_w, conv_b, dt_bias, a_log,
                 d_skip, ssm_norm_g, w_o_att, w_o_ssd, w_out):
    b, s, _ = h.shape
    proj = h @ w_in
    q, kv_lat, q_idx, k_idx, w_idx, z, xbc, dt_raw, g_att, g_ssd = _split_last(proj, IN_SPLITS)
    q = q.reshape(b, s, N_ATT_HEADS, ATT_HEAD_DIM)
    c_kv = rms_norm(kv_lat, kv_norm_g)
    q_idx = q_idx.reshape(b, s, IDX_HEADS, IDX_DIM)
    k_idx = rms_norm(k_idx, kidx_norm_g)
    w_idx = w_idx * (IDX_HEADS ** -0.5)
    att = dsa_attention(q, c_kv, q_idx, k_idx, w_idx, w_uk, w_uv)
    xbc = jax.nn.silu(causal_depthwise_conv(xbc, conv_w, conv_b))
    xs, bm, cm = _split_last(xbc, (D_INNER, N_SSM_GROUPS * D_STATE, N_SSM_GROUPS * D_STATE))
    xs = xs.reshape(b, s, N_SSM_GROUPS, HEADS_PER_GROUP, SSM_HEAD_DIM)
    bm = bm.reshape(b, s, N_SSM_GROUPS, D_STATE)
    cm = cm.reshape(b, s, N_SSM_GROUPS, D_STATE)
    dt = jax.nn.softplus((dt_raw + dt_bias).astype(jnp.float32)).reshape(b, s, N_SSM_GROUPS, HEADS_PER_GROUP)
    a = -jnp.exp(a_log.astype(jnp.float32)).reshape(N_SSM_GROUPS, HEADS_PER_GROUP)
    y = ssd_chunked(xs, dt, a, bm, cm).astype(h.dtype)
    y = y + xs * d_skip.reshape(N_SSM_GROUPS, HEADS_PER_GROUP, 1)
    y = y.reshape(b, s, D_INNER) * jax.nn.silu(z)
    y = _normalize(y.reshape(b, s, N_SSM_GROUPS, D_INNER // N_SSM_GROUPS)).reshape(b, s, D_INNER)
    y = (y * ssm_norm_g.astype(jnp.float32)).astype(h.dtype)
    branch_att = att @ w_o_att
    branch_ssd = y @ w_o_ssd
    merged = jax.nn.sigmoid(g_att) * branch_att + jax.nn.sigmoid(g_ssd) * branch_ssd
    return merged @ w_out


def squared_relu_mlp(h, w_up, w_down):
    return jnp.square(jax.nn.relu(h @ w_up)) @ w_down


def setup_inputs(seed: int = 0) -> dict:
    key = jax.random.key(seed)
    ks = jax.random.split(key, 24)
    L = DEPTH

    def nrm(k, shape, scale):
        return jax.random.normal(k, shape, jnp.float32) * scale

    dt_init = jnp.exp(jax.random.uniform(ks[12], (L, N_SSM_HEADS), jnp.float32,
                                         minval=math.log(1e-3), maxval=math.log(1e-1)))
    dt_bias = dt_init + jnp.log(-jnp.expm1(-dt_init))
    a_log = jnp.log(jax.random.uniform(ks[13], (L, N_SSM_HEADS), jnp.float32, minval=1.0, maxval=16.0))
    return {
        "x": nrm(ks[0], (BATCH, SEQ, D_MODEL), 1.0),
        "c": nrm(ks[1], (BATCH, D_MODEL), 1.0),
        "ada_w": nrm(ks[2], (L, D_MODEL, N_MOD * D_MODEL), D_MODEL ** -0.5),
        "ada_b": nrm(ks[3], (L, N_MOD * D_MODEL), 0.02),
        "norm_g": 1.0 + nrm(ks[4], (L, 4, D_MODEL), 0.05),
        "w_in": nrm(ks[5], (L, D_MODEL, D_IN_PROJ), D_MODEL ** -0.5),
        "kv_norm_g": 1.0 + nrm(ks[6], (L, KV_RANK), 0.05),
        "kidx_norm_g": 1.0 + nrm(ks[7], (L, IDX_DIM), 0.05),
        "w_uk": nrm(ks[8], (L, KV_RANK, N_ATT_HEADS, ATT_HEAD_DIM), KV_RANK ** -0.5),
        "w_uv": nrm(ks[9], (L, KV_RANK, N_ATT_HEADS, ATT_HEAD_DIM), KV_RANK ** -0.5),
        "conv_w": nrm(ks[10], (L, CONV_WIDTH, CONV_DIM), CONV_WIDTH ** -0.5),
        "conv_b": nrm(ks[11], (L, CONV_DIM), 0.02),
        "dt_bias": dt_bias,
        "a_log": a_log,
        "d_skip": 1.0 + nrm(ks[14], (L, N_SSM_HEADS), 0.1),
        "ssm_norm_g": 1.0 + nrm(ks[15], (L, D_INNER), 0.05),
        "w_o_att": nrm(ks[16], (L, ATT_WIDTH, D_MODEL), ATT_WIDTH ** -0.5),
        "w_o_ssd": nrm(ks[17], (L, D_INNER, D_MODEL), D_INNER ** -0.5),
        "w_out": nrm(ks[18], (L, D_MODEL, D_MODEL), D_MODEL ** -0.5),
        "w_up": nrm(ks[19], (L, D_MODEL, D_FF), D_MODEL ** -0.5),
        "w_down": nrm(ks[20], (L, D_FF, D_MODEL), D_FF ** -0.5),
    }


def reference(x, c, ada_w, ada_b, norm_g, w_in, kv_norm_g, kidx_norm_g, w_uk, w_uv, conv_w, conv_b,
              dt_bias, a_log, d_skip, ssm_norm_g, w_o_att, w_o_ssd, w_out, w_up, w_down):
    c_act = jax.nn.silu(c)
    for layer in range(DEPTH):
        mod = (c_act @ ada_w[layer] + ada_b[layer])[:, None, :]
        sh_m, sc_m, gt_m, sh_f, sc_f, gt_f = jnp.split(mod, N_MOD, axis=-1)
        h = rms_norm(x, norm_g[layer, 0]) * (1 + sc_m) + sh_m
        y = hybrid_mixer(h, w_in[layer], kv_norm_g[layer], kidx_norm_g[layer], w_uk[layer], w_uv[layer],
                         conv_w[layer], conv_b[layer], dt_bias[layer], a_log[layer], d_skip[layer],
                         ssm_norm_g[layer], w_o_att[layer], w_o_ssd[layer], w_out[layer])
        x = x + gt_m * rms_norm(y, norm_g[layer, 1])
        h = rms_norm(x, norm_g[layer, 2]) * (1 + sc_f) + sh_f
        y = squared_relu_mlp(h, w_up[layer], w_down[layer])
        x = x + gt_f * rms_norm(y, norm_g[layer, 3])
    return x
```

```python
import functools

import jax
import jax.numpy as jnp
import numpy as np
from jax import lax
from jax.experimental import pallas as pl
from jax.experimental.pallas import tpu as pltpu

F32 = jnp.float32
BF16 = jnp.bfloat16
I32 = jnp.int32

N_ATT_HEADS = 8
ATT_HEAD_DIM = 128
ATT_WIDTH = N_ATT_HEADS * ATT_HEAD_DIM
KV_RANK = 256
IDX_HEADS = 8
IDX_DIM = 64
TOPK_MAX = 256
D_STATE = 128
SSM_HEAD_DIM = 64
N_SSM_GROUPS = 4
HEADS_PER_GROUP = 8
N_SSM_HEADS = N_SSM_GROUPS * HEADS_PER_GROUP
D_INNER = N_SSM_HEADS * SSM_HEAD_DIM
BC_WIDTH = N_SSM_GROUPS * D_STATE
CONV_WIDTH = 4
N_MOD = 6
EPS = 1e-6

LANES = 128
SUBLANES = 8
VMEM_LIMIT_BYTES = 52 * 1024 * 1024

Q_TILE = 128
KEY_CHUNK = 256
SSD_CHUNK = 128
INPROJ_TM = 1024
INPROJ_TN = 512
OUT_TM = 512
MLP_TM = 1024
MLP_TF = 1024
MOD_TN = 1536

INT_MIN = -2147483648
NEG_BIG = -1e30

PB_Z, PB_XS, PB_Q, PB_GATT, PB_GSSD, PB_BM, PB_CM, PB_QIDX = (
    0, 2048, 4096, 5120, 6144, 7168, 7680, 8192)
PB_WIDTH = 8704
PF_KV, PF_KIDX, PF_DT = 0, 256, 384
PF_WIDTH = 512


def _params(*sem):
    return pltpu.CompilerParams(dimension_semantics=sem,
                                vmem_limit_bytes=VMEM_LIMIT_BYTES)


def _dot(a, b):
    return jnp.dot(a, b, preferred_element_type=F32)


def _dot_nt(a, b):
    return lax.dot_general(a, b, (((1,), (1,)), ((), ())),
                           preferred_element_type=F32)


def _rms_scale(x):
    return lax.rsqrt(jnp.mean(x * x, axis=-1, keepdims=True) + EPS)


def _mod_kernel(c_ref, w_ref, b_ref, o_ref):
    c = c_ref[...]
    c_act = (c * jax.nn.sigmoid(c)).astype(BF16)
    o_ref[...] = _dot(c_act, w_ref[...].astype(BF16)) + b_ref[...]


def _modulation(c, ada_w, ada_b):
    n_layers, d, n = ada_w.shape
    b = c.shape[0]
    rows = ((b + SUBLANES - 1) // SUBLANES) * SUBLANES
    c_pad = jnp.zeros((rows, d), F32).at[:b].set(c)
    out = pl.pallas_call(
        _mod_kernel,
        grid=(n_layers, n // MOD_TN),
        in_specs=[
            pl.BlockSpec((rows, d), lambda l, j: (0, 0)),
            pl.BlockSpec((None, d, MOD_TN), lambda l, j: (l, 0, j)),
            pl.BlockSpec((None, 1, MOD_TN), lambda l, j: (l, 0, j)),
        ],
        out_specs=pl.BlockSpec((None, rows, MOD_TN), lambda l, j: (l, 0, j)),
        out_shape=jax.ShapeDtypeStruct((n_layers, rows, n), F32),
        compiler_params=_params("arbitrary", "arbitrary"),
        name="adaln_mod",
    )(c_pad, ada_w, ada_b.reshape(n_layers, 1, n))
    return out[:, :b].reshape(n_layers, b, N_MOD, 1, d)


def _inproj_kernel(x_ref, g_ref, sc_ref, sh_ref, wb_ref, wf_ref,
                   ob_ref, of_ref, h_ref, *, n_bf16_tiles):
    j = pl.program_id(2)

    @pl.when(j == 0)
    def _():
        x = x_ref[...]
        h = x * _rms_scale(x) * g_ref[...]
        h = h * (1.0 + sc_ref[...]) + sh_ref[...]
        h_ref[...] = h.astype(BF16)

    @pl.when(j < n_bf16_tiles)
    def _():
        ob_ref[...] = _dot(h_ref[...], wb_ref[...]).astype(BF16)

    @pl.when(j == n_bf16_tiles)
    def _():
        of_ref[...] = _dot(h_ref[...], wf_ref[...])


def _input_projection(x, g, sc, sh, wb, wf):
    b, s, d = x.shape
    tm = min(INPROJ_TM, s)
    nj = PB_WIDTH // INPROJ_TN
    last = nj - 1
    return pl.pallas_call(
        functools.partial(_inproj_kernel, n_bf16_tiles=nj),
        grid=(b, s // tm, nj + 1),
        in_specs=[
            pl.BlockSpec((None, tm, d), lambda bi, i, j: (bi, i, 0)),
            pl.BlockSpec((1, d), lambda bi, i, j: (0, 0)),
            pl.BlockSpec((None, 1, d), lambda bi, i, j: (bi, 0, 0)),
            pl.BlockSpec((None, 1, d), lambda bi, i, j: (bi, 0, 0)),
            pl.BlockSpec((d, INPROJ_TN),
                         lambda bi, i, j: (0, jnp.minimum(j, last))),
            pl.BlockSpec((d, PF_WIDTH), lambda bi, i, j: (0, 0)),
        ],
        out_specs=[
            pl.BlockSpec((None, tm, INPROJ_TN),
                         lambda bi, i, j: (bi, i, jnp.minimum(j, last))),
            pl.BlockSpec((None, tm, PF_WIDTH), lambda bi, i, j: (bi, i, 0)),
        ],
        out_shape=[
            jax.ShapeDtypeStruct((b, s, PB_WIDTH), BF16),
            jax.ShapeDtypeStruct((b, s, PF_WIDTH), F32),
        ],
        scratch_shapes=[pltpu.VMEM((tm, d), BF16)],
        compiler_params=_params("arbitrary", "arbitrary", "arbitrary"),
        name="norm_inproj",
    )(x, g, sc, sh, wb, wf)


def _prep_kernel(kv_ref, kw_ref, gkv_ref, gk_ref, ckv_ref, ckvt_ref,
                 kidx_ref, wt_ref, *, n_chunks):
    kv = kv_ref[...]
    ckv = kv * _rms_scale(kv) * gkv_ref[...]
    ckv_ref[...] = ckv.astype(BF16)
    for c in range(n_chunks):
        blk = ckv[c * KEY_CHUNK:(c + 1) * KEY_CHUNK, :]
        ckvt_ref[c] = blk.T.astype(BF16)
    kw = kw_ref[...]
    k = kw[:, :IDX_DIM]
    kn = k * _rms_scale(k) * gk_ref[...]
    kidx_ref[...] = kn.astype(BF16)
    kwt = kw.T
    wt_ref[...] = kwt[IDX_DIM:IDX_DIM + IDX_HEADS, :] * (IDX_HEADS ** -0.5)


def _latent_prep(proj_f, kv_norm_g, kidx_norm_g):
    b, s, _ = proj_f.shape
    ts = min(1024, s)
    n_chunks = ts // KEY_CHUNK
    return pl.pallas_call(
        functools.partial(_prep_kernel, n_chunks=n_chunks),
        grid=(b, s // ts),
        in_specs=[
            pl.BlockSpec((None, ts, KV_RANK),
                         lambda bi, i: (bi, i, PF_KV // KV_RANK)),
            pl.BlockSpec((None, ts, LANES),
                         lambda bi, i: (bi, i, PF_KIDX // LANES)),
            pl.BlockSpec((1, KV_RANK), lambda bi, i: (0, 0)),
            pl.BlockSpec((1, IDX_DIM), lambda bi, i: (0, 0)),
        ],
        out_specs=[
            pl.BlockSpec((None, ts, KV_RANK), lambda bi, i: (bi, i, 0)),
            pl.BlockSpec((None, n_chunks, KV_RANK, KEY_CHUNK),
                         lambda bi, i: (bi, i, 0, 0)),
            pl.BlockSpec((None, ts, IDX_DIM), lambda bi, i: (bi, i, 0)),
            pl.BlockSpec((None, IDX_HEADS, ts), lambda bi, i: (bi, 0, i)),
        ],
        out_shape=[
            jax.ShapeDtypeStruct((b, s, KV_RANK), BF16),
            jax.ShapeDtypeStruct((b, s // KEY_CHUNK, KV_RANK, KEY_CHUNK), BF16),
            jax.ShapeDtypeStruct((b, s, IDX_DIM), BF16),
            jax.ShapeDtypeStruct((b, IDX_HEADS, s), F32),
        ],
        compiler_params=_params("arbitrary", "arbitrary"),
        name="latent_prep",
    )(proj_f, proj_f, kv_norm_g, kidx_norm_g)


def _dsa_kernel(q_ref, qidx_ref, wt_ref, ckv_ref, ckvt_ref, kidx_ref,
                wuk_ref, wuvt_ref, o_ref,
                keys_ref, qall_ref, qabs_ref, acc_ref, tsel_ref, *, topk):
    qb = pl.program_id(1)
    n_chunks = (qb * Q_TILE + Q_TILE + KEY_CHUNK - 1) // KEY_CHUNK
    rows = KEY_CHUNK // SUBLANES
    q_pos = qb * Q_TILE + lax.broadcasted_iota(I32, (KEY_CHUNK, Q_TILE), 1)
    row_iota = lax.broadcasted_iota(I32, (KEY_CHUNK, Q_TILE), 0)

    for h in range(IDX_HEADS):
        qall_ref[h * Q_TILE:(h + 1) * Q_TILE, :] = (
            qidx_ref[:, h * IDX_DIM:(h + 1) * IDX_DIM])

    def score_chunk(c, carry):
        start = pl.multiple_of(c * KEY_CHUNK, KEY_CHUNK)
        ks = kidx_ref[pl.ds(start, KEY_CHUNK), :]
        logits = _dot_nt(ks, qall_ref[...])
        score = jnp.zeros((KEY_CHUNK, Q_TILE), F32)
        for h in range(IDX_HEADS):
            lg = logits[:, h * Q_TILE:(h + 1) * Q_TILE]
            score = score + wt_ref[h:h + 1, :] * jnp.maximum(lg, 0.0)
        bits = pltpu.bitcast(score, I32)
        key = bits ^ ((bits >> 31) & 0x7FFFFFFF)
        key = jnp.where(start + row_iota > q_pos, INT_MIN, key)
        keys_ref[pl.ds(start, KEY_CHUNK), :] = key
        return carry

    lax.fori_loop(0, n_chunks, score_chunk, 0)

    def fold_rows(m):
        part = m[0:SUBLANES]
        for r in range(1, rows):
            part = part + m[r * SUBLANES:(r + 1) * SUBLANES]
        return part

    def count(pred):
        def body(c, cnt):
            start = pl.multiple_of(c * KEY_CHUNK, KEY_CHUNK)
            kk = keys_ref[pl.ds(start, KEY_CHUNK), :]
            return cnt + fold_rows(jnp.where(pred(kk, start), 1, 0))
        cnt = lax.fori_loop(0, n_chunks, body, jnp.zeros((SUBLANES, Q_TILE), I32))
        return jnp.sum(cnt, axis=0, keepdims=True)

    zero = jnp.zeros((1, Q_TILE), I32)
    cnt0 = count(lambda kk, start: kk >= zero)
    cand = jnp.where(cnt0 >= topk, zero, jnp.full((1, Q_TILE), INT_MIN, I32))

    def bit_step(i, cand):
        trial = cand | (jnp.int32(1) << (30 - i))
        cnt = count(lambda kk, start: kk >= trial)
        return jnp.where(cnt >= topk, trial, cand)

    thr = lax.fori_loop(0, 31, bit_step, cand)

    cnt_gt = count(lambda kk, start: kk > thr)
    cnt_eq = count(lambda kk, start: kk == thr)
    need = topk - cnt_gt
    real = thr != INT_MIN
    tsel_ref[...] = jnp.where(real, jnp.int32(2 ** 30), jnp.int32(-1))
    n_tied = jnp.max(jnp.where(real & (cnt_eq > need), 1, 0))

    @pl.when(n_tied > 0)
    def _():
        def pos_step(i, t):
            trial = t | (jnp.int32(1) << (29 - i))
            cnt = count(lambda kk, start:
                        (kk == thr) & (start + row_iota < trial))
            return jnp.where(cnt < need, trial, t)
        t = lax.fori_loop(0, 30, pos_step, zero)
        tsel_ref[...] = jnp.where(real, t, jnp.int32(-1))

    tsel = tsel_ref[...]

    scale = ATT_HEAD_DIM ** -0.5
    for h in range(N_ATT_HEADS):
        qh = q_ref[:, h * ATT_HEAD_DIM:(h + 1) * ATT_HEAD_DIM]
        qa = _dot_nt(wuk_ref[h], qh) * scale
        qabs_ref[:, h * Q_TILE:(h + 1) * Q_TILE] = qa.astype(BF16)

    width = N_ATT_HEADS * Q_TILE
    acc_ref[...] = jnp.zeros_like(acc_ref)

    def attn_chunk(c, carry):
        m, l = carry
        start = pl.multiple_of(c * KEY_CHUNK, KEY_CHUNK)
        kv = ckv_ref[pl.ds(start, KEY_CHUNK), :]
        st = _dot(kv, qabs_ref[...])
        kk = keys_ref[pl.ds(start, KEY_CHUNK), :]
        pos = start + row_iota
        bias = jnp.where(
            kk > thr, 0.0,
            jnp.where(kk == thr, jnp.where(pos <= tsel, 0.0, NEG_BIG), NEG_BIG))
        st = st + jnp.concatenate([bias] * N_ATT_HEADS, axis=1)
        m_new = jnp.maximum(m, jnp.max(st, axis=0, keepdims=True))
        alpha = jnp.exp(m - m_new)
        p = jnp.exp(st - m_new)
        l_new = alpha * l + jnp.sum(p, axis=0, keepdims=True)
        pv = _dot(ckvt_ref[c], p.astype(BF16))
        acc_ref[...] = acc_ref[...] * alpha + pv
        return m_new, l_new

    m0 = jnp.full((1, width), NEG_BIG, F32)
    l0 = jnp.zeros((1, width), F32)
    _, l = lax.fori_loop(0, n_chunks, attn_chunk, (m0, l0))

    o_lat = (acc_ref[...] * (1.0 / l)).astype(BF16)
    for h in range(N_ATT_HEADS):
        ot = _dot(wuvt_ref[h], o_lat[:, h * Q_TILE:(h + 1) * Q_TILE])
        o_ref[:, h * ATT_HEAD_DIM:(h + 1) * ATT_HEAD_DIM] = ot.T.astype(BF16)


def _dsa_attention(proj_b, wt, ckv, ckvt, kidx, wuk, wuvt):
    b, s, _ = proj_b.shape
    topk = min(TOPK_MAX, s // 4)
    width = N_ATT_HEADS * Q_TILE
    return pl.pallas_call(
        functools.partial(_dsa_kernel, topk=topk),
        grid=(b, s // Q_TILE),
        in_specs=[
            pl.BlockSpec((None, Q_TILE, ATT_WIDTH),
                         lambda bi, i: (bi, i, PB_Q // ATT_WIDTH)),
            pl.BlockSpec((None, Q_TILE, IDX_HEADS * IDX_DIM),
                         lambda bi, i: (bi, i, PB_QIDX // (IDX_HEADS * IDX_DIM))),
            pl.BlockSpec((None, IDX_HEADS, Q_TILE), lambda bi, i: (bi, 0, i)),
            pl.BlockSpec((None, s, KV_RANK), lambda bi, i: (bi, 0, 0)),
            pl.BlockSpec((None, s // KEY_CHUNK, KV_RANK, KEY_CHUNK),
                         lambda bi, i: (bi, 0, 0, 0)),
            pl.BlockSpec((None, s, IDX_DIM), lambda bi, i: (bi, 0, 0)),
            pl.BlockSpec((N_ATT_HEADS, KV_RANK, ATT_HEAD_DIM),
                         lambda bi, i: (0, 0, 0)),
            pl.BlockSpec((N_ATT_HEADS, ATT_HEAD_DIM, KV_RANK),
                         lambda bi, i: (0, 0, 0)),
        ],
        out_specs=pl.BlockSpec((None, Q_TILE, ATT_WIDTH), lambda bi, i: (bi, i, 0)),
        out_shape=jax.ShapeDtypeStruct((b, s, ATT_WIDTH), BF16),
        scratch_shapes=[
            pltpu.VMEM((s, Q_TILE), I32),
            pltpu.VMEM((IDX_HEADS * Q_TILE, IDX_DIM), BF16),
            pltpu.VMEM((KV_RANK, width), BF16),
            pltpu.VMEM((KV_RANK, width), F32),
            pltpu.VMEM((1, Q_TILE), I32),
        ],
        compiler_params=_params("arbitrary", "arbitrary"),
        name="dsa_attention",
    )(proj_b, proj_b, wt, ckv, ckvt, kidx, wuk, wuvt)


def _ssd_kernel(z_ref, xs_ref, bm_ref, cm_ref, dt_ref,
                cwx_ref, cwb_ref, cwc_ref, cbx_ref, cbb_ref, cbc_ref,
                dtb_ref, alog_ref, dskip_ref, ng_ref, y_ref,
                ex_ref, eb_ref, ec_ref, state_ref, yacc_ref, *, chunk):
    ci = pl.program_id(1)
    halo = SUBLANES

    @pl.when(ci == 0)
    def _():
        ex_ref[0:halo, :] = jnp.zeros((halo, D_INNER), F32)
        eb_ref[0:halo, :] = jnp.zeros((halo, BC_WIDTH), F32)
        ec_ref[0:halo, :] = jnp.zeros((halo, BC_WIDTH), F32)
        state_ref[...] = jnp.zeros_like(state_ref)

    def conv_silu(in_ref, ext_ref, w_ref, b_ref):
        ext_ref[halo:halo + chunk, :] = in_ref[...].astype(F32)
        acc = b_ref[...]
        for k in range(CONV_WIDTH):
            off = halo - (CONV_WIDTH - 1) + k
            acc = acc + w_ref[k:k + 1, :] * ext_ref[pl.ds(off, chunk), :]
        ext_ref[0:halo, :] = ext_ref[chunk:chunk + halo, :]
        return acc * jax.nn.sigmoid(acc)

    xs = conv_silu(xs_ref, ex_ref, cwx_ref, cbx_ref)
    bm = conv_silu(bm_ref, eb_ref, cwb_ref, cbb_ref)
    cm = conv_silu(cm_ref, ec_ref, cwc_ref, cbc_ref)
    xs_b = xs.astype(BF16)

    dt = jax.nn.softplus(dt_ref[...] + dtb_ref[...])
    a = -jnp.exp(alog_ref[...])
    r_io = lax.broadcasted_iota(I32, (chunk, chunk), 0)
    c_io = lax.broadcasted_iota(I32, (chunk, chunk), 1)
    tril = r_io >= c_io
    a_cum = jnp.dot(tril.astype(F32), dt * a, precision=lax.Precision.HIGHEST,
                    preferred_element_type=F32)
    total = a_cum[chunk - 1:chunk, :]
    to_end = jnp.exp(total - a_cum) * dt
    a_cum_t = a_cum.T
    dt_t = dt.T
    to_end_t = to_end.T

    for g in range(N_SSM_GROUPS):
        bg = bm[:, g * D_STATE:(g + 1) * D_STATE]
        cg = cm[:, g * D_STATE:(g + 1) * D_STATE]
        cb = _dot_nt(cg.astype(BF16), bg.astype(BF16))
        bg_t = bg.T
        for hh in range(HEADS_PER_GROUP):
            h = g * HEADS_PER_GROUP + hh
            col = jnp.broadcast_to(a_cum[:, h:h + 1], (chunk, chunk))
            row = a_cum_t[h:h + 1, :]
            decay = jnp.exp(jnp.where(tril, col - row, -jnp.inf))
            m_h = cb * decay * dt_t[h:h + 1, :]
            if chunk == D_STATE:
                ecol = jnp.exp(col)
            else:
                ecol = jnp.exp(jnp.broadcast_to(a_cum[:, h:h + 1], (chunk, D_STATE)))
            c_h = cg * ecol
            lhs = jnp.concatenate([m_h.astype(BF16), c_h.astype(BF16)], axis=1)
            xs_h = xs_b[:, h * SSM_HEAD_DIM:(h + 1) * SSM_HEAD_DIM]
            lo, hi = h * SSM_HEAD_DIM, (h + 1) * SSM_HEAD_DIM
            prev = state_ref[:, lo:hi]
            rhs = jnp.concatenate([xs_h, prev.astype(BF16)], axis=0)
            yacc_ref[:, lo:hi] = _dot(lhs, rhs)
            new = _dot((bg_t * to_end_t[h:h + 1, :]).astype(BF16), xs_h)
            state_ref[:, lo:hi] = prev * ecol[chunk - 1:chunk, :SSM_HEAD_DIM] + new

    y = yacc_ref[...] + xs * dskip_ref[...]
    zf = z_ref[...].astype(F32)
    y = y * (zf * jax.nn.sigmoid(zf))
    gw = D_INNER // N_SSM_GROUPS
    for g in range(N_SSM_GROUPS):
        yg = y[:, g * gw:(g + 1) * gw]
        yn = yg * _rms_scale(yg) * ng_ref[:, g * gw:(g + 1) * gw]
        y_ref[:, g * gw:(g + 1) * gw] = yn.astype(BF16)


def _ssd(proj_b, proj_f, conv_w, conv_b, dt_bias, a_log, d_skip, ssm_norm_g):
    b, s, _ = proj_b.shape
    chunk = min(SSD_CHUNK, s)
    pad = LANES - N_SSM_HEADS
    cw_x, cw_b, cw_c = (conv_w[:, :D_INNER], conv_w[:, D_INNER:D_INNER + BC_WIDTH],
                        conv_w[:, D_INNER + BC_WIDTH:])
    cb = conv_b.reshape(1, -1)
    cb_x, cb_b, cb_c = (cb[:, :D_INNER], cb[:, D_INNER:D_INNER + BC_WIDTH],
                        cb[:, D_INNER + BC_WIDTH:])
    dtb = jnp.pad(dt_bias, (0, pad)).reshape(1, LANES)
    alog = jnp.pad(a_log, (0, pad)).reshape(1, LANES)
    dskip = jnp.repeat(d_skip, SSM_HEAD_DIM).reshape(1, D_INNER)
    const = lambda bi, i: (0, 0)
    return pl.pallas_call(
        functools.partial(_ssd_kernel, chunk=chunk),
        grid=(b, s // chunk),
        in_specs=[
            pl.BlockSpec((None, chunk, D_INNER), lambda bi, i: (bi, i, PB_Z // D_INNER)),
            pl.BlockSpec((None, chunk, D_INNER), lambda bi, i: (bi, i, PB_XS // D_INNER)),
            pl.BlockSpec((None, chunk, BC_WIDTH), lambda bi, i: (bi, i, PB_BM // BC_WIDTH)),
            pl.BlockSpec((None, chunk, BC_WIDTH), lambda bi, i: (bi, i, PB_CM // BC_WIDTH)),
            pl.BlockSpec((None, chunk, LANES), lambda bi, i: (bi, i, PF_DT // LANES)),
            pl.BlockSpec((CONV_WIDTH, D_INNER), const),
            pl.BlockSpec((CONV_WIDTH, BC_WIDTH), const),
            pl.BlockSpec((CONV_WIDTH, BC_WIDTH), const),
            pl.BlockSpec((1, D_INNER), const),
            pl.BlockSpec((1, BC_WIDTH), const),
            pl.BlockSpec((1, BC_WIDTH), const),
            pl.BlockSpec((1, LANES), const),
            pl.BlockSpec((1, LANES), const),
            pl.BlockSpec((1, D_INNER), const),
            pl.BlockSpec((1, D_INNER), const),
        ],
        out_specs=pl.BlockSpec((None, chunk, D_INNER), lambda bi, i: (bi, i, 0)),
        out_shape=jax.ShapeDtypeStruct((b, s, D_INNER), BF16),
        scratch_shapes=[
            pltpu.VMEM((chunk + 2 * SUBLANES, D_INNER), F32),
            pltpu.VMEM((chunk + 2 * SUBLANES, BC_WIDTH), F32),
            pltpu.VMEM((chunk + 2 * SUBLANES, BC_WIDTH), F32),
            pltpu.VMEM((D_STATE, D_INNER), F32),
            pltpu.VMEM((chunk, D_INNER), F32),
        ],
        compiler_params=_params("arbitrary", "arbitrary"),
        name="ssd_mixer",
    )(proj_b, proj_b, proj_b, proj_b, proj_f, cw_x, cw_b, cw_c, cb_x, cb_b, cb_c,
      dtb, alog, dskip, ssm_norm_g.reshape(1, D_INNER))


def _mix_out_kernel(att_ref, y_ref, ga_ref, gs_ref, x_ref, gate_ref, g_ref,
                    woa_ref, wos_ref, wout_ref, o_ref):
    ba = _dot(att_ref[...], woa_ref[...])
    bs = _dot(y_ref[...], wos_ref[...])
    merged = (jax.nn.sigmoid(ga_ref[...].astype(F32)) * ba
              + jax.nn.sigmoid(gs_ref[...].astype(F32)) * bs)
    out = _dot(merged.astype(BF16), wout_ref[...])
    normed = out * _rms_scale(out) * g_ref[...]
    o_ref[...] = x_ref[...] + gate_ref[...] * normed


def _mix_out(att, y, proj_b, x, gate, g, woa, wos, wout):
    b, s, d = x.shape
    tm = min(OUT_TM, s)
    const = lambda bi, i: (0, 0)
    return pl.pallas_call(
        _mix_out_kernel,
        grid=(b, s // tm),
        in_specs=[
            pl.BlockSpec((None, tm, ATT_WIDTH), lambda bi, i: (bi, i, 0)),
            pl.BlockSpec((None, tm, D_INNER), lambda bi, i: (bi, i, 0)),
            pl.BlockSpec((None, tm, d), lambda bi, i: (bi, i, PB_GATT // d)),
            pl.BlockSpec((None, tm, d), lambda bi, i: (bi, i, PB_GSSD // d)),
            pl.BlockSpec((None, tm, d), lambda bi, i: (bi, i, 0)),
            pl.BlockSpec((None, 1, d), lambda bi, i: (bi, 0, 0)),
            pl.BlockSpec((1, d), const),
            pl.BlockSpec((ATT_WIDTH, d), const),
            pl.BlockSpec((D_INNER, d), const),
            pl.BlockSpec((d, d), const),
        ],
        out_specs=pl.BlockSpec((None, tm, d), lambda bi, i: (bi, i, 0)),
        out_shape=jax.ShapeDtypeStruct((b, s, d), F32),
        compiler_params=_params("arbitrary", "arbitrary"),
        name="mix_out",
    )(att, y, proj_b, proj_b, x, gate, g, woa, wos, wout)


def _mlp_kernel(x_ref, g_in_ref, sc_ref, sh_ref, wup_ref, wdn_ref, gate_ref,
                g_out_ref, o_ref, h_ref, acc_ref):
    j = pl.program_id(2)

    @pl.when(j == 0)
    def _():
        x = x_ref[...]
        h = x * _rms_scale(x) * g_in_ref[...]
        h = h * (1.0 + sc_ref[...]) + sh_ref[...]
        h_ref[...] = h.astype(BF16)
        acc_ref[...] = jnp.zeros_like(acc_ref)

    up = jnp.maximum(_dot(h_ref[...], wup_ref[...]), 0.0)
    acc_ref[...] += _dot((up * up).astype(BF16), wdn_ref[...])

    @pl.when(j == pl.num_programs(2) - 1)
    def _():
        y = acc_ref[...]
        normed = y * _rms_scale(y) * g_out_ref[...]
        o_ref[...] = x_ref[...] + gate_ref[...] * normed


def _mlp(x, g_in, sc, sh, wup, wdn, gate, g_out):
    b, s, d = x.shape
    ff = wup.shape[1]
    tm = min(MLP_TM, s)
    per_b = lambda bi, i, j: (bi, 0, 0)
    const = lambda bi, i, j: (0, 0)
    return pl.pallas_call(
        _mlp_kernel,
        grid=(b, s // tm, ff // MLP_TF),
        in_specs=[
            pl.BlockSpec((None, tm, d), lambda bi, i, j: (bi, i, 0)),
            pl.BlockSpec((1, d), const),
            pl.BlockSpec((None, 1, d), per_b),
            pl.BlockSpec((None, 1, d), per_b),
            pl.BlockSpec((d, MLP_TF), lambda bi, i, j: (0, j)),
            pl.BlockSpec((MLP_TF, d), lambda bi, i, j: (j, 0)),
            pl.BlockSpec((None, 1, d), per_b),
            pl.BlockSpec((1, d), const),
        ],
        out_specs=pl.BlockSpec((None, tm, d), lambda bi, i, j: (bi, i, 0)),
        out_shape=jax.ShapeDtypeStruct((b, s, d), F32),
        scratch_shapes=[pltpu.VMEM((tm, d), BF16), pltpu.VMEM((tm, d), F32)],
        compiler_params=_params("arbitrary", "arbitrary", "arbitrary"),
        name="mlp",
    )(x, g_in, sc, sh, wup, wdn, gate, g_out)


def _split_offsets():
    sizes = (ATT_WIDTH, KV_RANK, IDX_HEADS * IDX_DIM, IDX_DIM, IDX_HEADS,
             D_INNER, D_INNER, BC_WIDTH, BC_WIDTH, N_SSM_HEADS, ATT_WIDTH, ATT_WIDTH)
    names = ("q", "kv", "qidx", "kidx", "widx", "z", "xs", "bm", "cm", "dt",
             "gatt", "gssd")
    offs = np.concatenate([[0], np.cumsum(sizes)])
    return {n: (int(offs[i]), int(offs[i + 1])) for i, n in enumerate(names)}


def _pack_w_in(w_in):
    sl = _split_offsets()
    col = lambda n: w_in[..., sl[n][0]:sl[n][1]]
    wb = jnp.concatenate([col(n) for n in
                          ("z", "xs", "q", "gatt", "gssd", "bm", "cm", "qidx")],
                         axis=-1).astype(BF16)
    lead = w_in.shape[:-1]
    zeros = lambda n: jnp.zeros(lead + (n,), w_in.dtype)
    wf = jnp.concatenate([
        col("kv"), col("kidx"), col("widx"), zeros(LANES - IDX_DIM - IDX_HEADS),
        col("dt"), zeros(LANES - N_SSM_HEADS)], axis=-1).astype(BF16)
    return wb, wf


def kernel(x, c, ada_w, ada_b, norm_g, w_in, kv_norm_g, kidx_norm_g, w_uk, w_uv,
           conv_w, conv_b, dt_bias, a_log, d_skip, ssm_norm_g, w_o_att, w_o_ssd,
           w_out, w_up, w_down):
    depth = ada_w.shape[0]
    d = x.shape[-1]
    mod = _modulation(c, ada_w, ada_b)
    wb_all, wf_all = _pack_w_in(w_in)
    wuk_all = jnp.transpose(w_uk, (0, 2, 1, 3)).astype(BF16)
    wuvt_all = jnp.transpose(w_uv, (0, 2, 3, 1)).astype(BF16)
    woa_all = w_o_att.astype(BF16)
    wos_all = w_o_ssd.astype(BF16)
    wout_all = w_out.astype(BF16)
    wup_all = w_up.astype(BF16)
    wdn_all = w_down.astype(BF16)

    for layer in range(depth):
        sh_m, sc_m, gt_m, sh_f, sc_f, gt_f = (mod[layer, :, k] for k in range(N_MOD))
        ng = lambda k: norm_g[layer, k].reshape(1, d)
        proj_b, proj_f = _input_projection(x, ng(0), sc_m, sh_m,
                                           wb_all[layer], wf_all[layer])
        ckv, ckvt, kidx, wt = _latent_prep(
            proj_f, kv_norm_g[layer].reshape(1, KV_RANK),
            kidx_norm_g[layer].reshape(1, IDX_DIM))
        att = _dsa_attention(proj_b, wt, ckv, ckvt, kidx,
                             wuk_all[layer], wuvt_all[layer])
        y = _ssd(proj_b, proj_f, conv_w[layer], conv_b[layer], dt_bias[layer],
                 a_log[layer], d_skip[layer], ssm_norm_g[layer])
        x = _mix_out(att, y, proj_b, x, gt_m, ng(1), woa_all[layer],
                     wos_all[layer], wout_all[layer])
        x = _mlp(x, ng(2), sc_f, sh_f, wup_all[layer], wdn_all[layer], gt_f, ng(3))
    return x
```

```python
import functools

import jax
import jax.numpy as jnp
import numpy as np
from jax import lax
from jax.experimental import pallas as pl
from jax.experimental.pallas import tpu as pltpu

F32 = jnp.float32
BF16 = jnp.bfloat16
I32 = jnp.int32

N_ATT_HEADS = 8
ATT_HEAD_DIM = 128
ATT_WIDTH = N_ATT_HEADS * ATT_HEAD_DIM
KV_RANK = 256
IDX_HEADS = 8
IDX_DIM = 64
TOPK_MAX = 256
D_STATE = 128
SSM_HEAD_DIM = 64
N_SSM_GROUPS = 4
HEADS_PER_GROUP = 8
N_SSM_HEADS = N_SSM_GROUPS * HEADS_PER_GROUP
D_INNER = N_SSM_HEADS * SSM_HEAD_DIM
BC_WIDTH = N_SSM_GROUPS * D_STATE
CONV_WIDTH = 4
N_MOD = 6
EPS = 1e-6

LANES = 128
SUBLANES = 8
VMEM_LIMIT_BYTES = 52 * 1024 * 1024

Q_TILE = 128
KEY_CHUNK = 256
SSD_CHUNK = 128
INPROJ_TM = 1024
INPROJ_TN = 512
OUT_TM = 512
MLP_TM = 1024
MLP_TF = 1024
MOD_TN = 1536

INT_MIN = -2147483648
LOG2_E = 1.4426950408889634
COUNT_ACCS = 8
NEG_BIG = -1e30

PB_Z, PB_XS, PB_Q, PB_GATT, PB_GSSD, PB_BM, PB_CM, PB_QIDX = (
    0, 2048, 4096, 5120, 6144, 7168, 7680, 8192)
PB_WIDTH = 8704
PF_KV, PF_KIDX, PF_DT = 0, 256, 384
PF_WIDTH = 512


def _params(*sem):
    return pltpu.CompilerParams(dimension_semantics=sem,
                                vmem_limit_bytes=VMEM_LIMIT_BYTES)


def _dot(a, b):
    return jnp.dot(a, b, preferred_element_type=F32)


def _dot_nt(a, b):
    return lax.dot_general(a, b, (((1,), (1,)), ((), ())),
                           preferred_element_type=F32)


def _rms_scale(x):
    return lax.rsqrt(jnp.mean(x * x, axis=-1, keepdims=True) + EPS)


def _mod_kernel(c_ref, w_ref, b_ref, o_ref):
    c = c_ref[...]
    c_act = (c * jax.nn.sigmoid(c)).astype(BF16)
    o_ref[...] = _dot(c_act, w_ref[...].astype(BF16)) + b_ref[...]


def _modulation(c, ada_w, ada_b):
    n_layers, d, n = ada_w.shape
    b = c.shape[0]
    rows = ((b + SUBLANES - 1) // SUBLANES) * SUBLANES
    c_pad = jnp.zeros((rows, d), F32).at[:b].set(c)
    out = pl.pallas_call(
        _mod_kernel,
        grid=(n_layers, n // MOD_TN),
        in_specs=[
            pl.BlockSpec((rows, d), lambda l, j: (0, 0)),
            pl.BlockSpec((None, d, MOD_TN), lambda l, j: (l, 0, j)),
            pl.BlockSpec((None, 1, MOD_TN), lambda l, j: (l, 0, j)),
        ],
        out_specs=pl.BlockSpec((None, rows, MOD_TN), lambda l, j: (l, 0, j)),
        out_shape=jax.ShapeDtypeStruct((n_layers, rows, n), F32),
        compiler_params=_params("arbitrary", "arbitrary"),
        name="adaln_mod",
    )(c_pad, ada_w, ada_b.reshape(n_layers, 1, n))
    return out[:, :b].reshape(n_layers, b, N_MOD, 1, d)


def _inproj_kernel(x_ref, g_ref, sc_ref, sh_ref, wb_ref, wf_ref,
                   ob_ref, of_ref, h_ref, *, n_bf16_tiles):
    j = pl.program_id(2)

    @pl.when(j == 0)
    def _():
        x = x_ref[...]
        h = x * _rms_scale(x) * g_ref[...]
        h = h * (1.0 + sc_ref[...]) + sh_ref[...]
        h_ref[...] = h.astype(BF16)

    @pl.when(j < n_bf16_tiles)
    def _():
        ob_ref[...] = _dot(h_ref[...], wb_ref[...]).astype(BF16)

    @pl.when(j == n_bf16_tiles)
    def _():
        of_ref[...] = _dot(h_ref[...], wf_ref[...])


def _input_projection(x, g, sc, sh, wb, wf):
    b, s, d = x.shape
    tm = min(INPROJ_TM, s)
    nj = PB_WIDTH // INPROJ_TN
    last = nj - 1
    return pl.pallas_call(
        functools.partial(_inproj_kernel, n_bf16_tiles=nj),
        grid=(b, s // tm, nj + 1),
        in_specs=[
            pl.BlockSpec((None, tm, d), lambda bi, i, j: (bi, i, 0)),
            pl.BlockSpec((1, d), lambda bi, i, j: (0, 0)),
            pl.BlockSpec((None, 1, d), lambda bi, i, j: (bi, 0, 0)),
            pl.BlockSpec((None, 1, d), lambda bi, i, j: (bi, 0, 0)),
            pl.BlockSpec((d, INPROJ_TN),
                         lambda bi, i, j: (0, jnp.minimum(j, last))),
            pl.BlockSpec((d, PF_WIDTH), lambda bi, i, j: (0, 0)),
        ],
        out_specs=[
            pl.BlockSpec((None, tm, INPROJ_TN),
                         lambda bi, i, j: (bi, i, jnp.minimum(j, last))),
            pl.BlockSpec((None, tm, PF_WIDTH), lambda bi, i, j: (bi, i, 0)),
        ],
        out_shape=[
            jax.ShapeDtypeStruct((b, s, PB_WIDTH), BF16),
            jax.ShapeDtypeStruct((b, s, PF_WIDTH), F32),
        ],
        scratch_shapes=[pltpu.VMEM((tm, d), BF16)],
        compiler_params=_params("arbitrary", "arbitrary", "arbitrary"),
        name="norm_inproj",
    )(x, g, sc, sh, wb, wf)


def _prep_kernel(kv_ref, kw_ref, gkv_ref, gk_ref, ckv_ref, ckvt_ref,
                 kidx_ref, wt_ref, *, n_chunks):
    kv = kv_ref[...]
    ckv = kv * _rms_scale(kv) * gkv_ref[...]
    ckv_ref[...] = ckv.astype(BF16)
    for c in range(n_chunks):
        blk = ckv[c * KEY_CHUNK:(c + 1) * KEY_CHUNK, :]
        ckvt_ref[c] = blk.T.astype(BF16)
    kw = kw_ref[...]
    k = kw[:, :IDX_DIM]
    kn = k * _rms_scale(k) * gk_ref[...]
    kidx_ref[...] = kn.astype(BF16)
    kwt = kw.T
    wt_ref[...] = kwt[IDX_DIM:IDX_DIM + IDX_HEADS, :] * (IDX_HEADS ** -0.5)


def _latent_prep(proj_f, kv_norm_g, kidx_norm_g):
    b, s, _ = proj_f.shape
    ts = min(1024, s)
    n_chunks = ts // KEY_CHUNK
    return pl.pallas_call(
        functools.partial(_prep_kernel, n_chunks=n_chunks),
        grid=(b, s // ts),
        in_specs=[
            pl.BlockSpec((None, ts, KV_RANK),
                         lambda bi, i: (bi, i, PF_KV // KV_RANK)),
            pl.BlockSpec((None, ts, LANES),
                         lambda bi, i: (bi, i, PF_KIDX // LANES)),
            pl.BlockSpec((1, KV_RANK), lambda bi, i: (0, 0)),
            pl.BlockSpec((1, IDX_DIM), lambda bi, i: (0, 0)),
        ],
        out_specs=[
            pl.BlockSpec((None, ts, KV_RANK), lambda bi, i: (bi, i, 0)),
            pl.BlockSpec((None, n_chunks, KV_RANK, KEY_CHUNK),
                         lambda bi, i: (bi, i, 0, 0)),
            pl.BlockSpec((None, ts, IDX_DIM), lambda bi, i: (bi, i, 0)),
            pl.BlockSpec((None, IDX_HEADS, ts), lambda bi, i: (bi, 0, i)),
        ],
        out_shape=[
            jax.ShapeDtypeStruct((b, s, KV_RANK), BF16),
            jax.ShapeDtypeStruct((b, s // KEY_CHUNK, KV_RANK, KEY_CHUNK), BF16),
            jax.ShapeDtypeStruct((b, s, IDX_DIM), BF16),
            jax.ShapeDtypeStruct((b, IDX_HEADS, s), F32),
        ],
        compiler_params=_params("arbitrary", "arbitrary"),
        name="latent_prep",
    )(proj_f, proj_f, kv_norm_g, kidx_norm_g)


def _dsa_kernel(q_ref, qidx_ref, wt_ref, ckv_ref, ckvt_ref, kidx_ref,
                wuk_ref, wuvt_ref, o_ref,
                keys_ref, qall_ref, qabs_ref, acc_ref, tsel_ref, p_ref,
                *, topk, pos_bits):
    qb = pl.program_id(1)
    n_chunks = (qb * Q_TILE + Q_TILE + KEY_CHUNK - 1) // KEY_CHUNK
    q_pos = qb * Q_TILE + lax.broadcasted_iota(I32, (KEY_CHUNK, Q_TILE), 1)
    row_iota = lax.broadcasted_iota(I32, (KEY_CHUNK, Q_TILE), 0)

    for h in range(IDX_HEADS):
        qall_ref[h * Q_TILE:(h + 1) * Q_TILE, :] = (
            qidx_ref[:, h * IDX_DIM:(h + 1) * IDX_DIM])

    def score_chunk(c, carry):
        start = pl.multiple_of(c * KEY_CHUNK, KEY_CHUNK)
        ks = kidx_ref[pl.ds(start, KEY_CHUNK), :]
        logits = _dot_nt(ks, qall_ref[...])
        score = jnp.zeros((KEY_CHUNK, Q_TILE), F32)
        for h in range(IDX_HEADS):
            lg = logits[:, h * Q_TILE:(h + 1) * Q_TILE]
            score = score + wt_ref[h:h + 1, :] * jnp.maximum(lg, 0.0)
        bits = pltpu.bitcast(score, I32)
        key = bits ^ ((bits >> 31) & 0x7FFFFFFF)
        key = jnp.where(start + row_iota > q_pos, INT_MIN, key)
        keys_ref[pl.ds(start, KEY_CHUNK), :] = key
        return carry

    lax.fori_loop(0, n_chunks, score_chunk, 0)

    n_pairs = (n_chunks + 1) // 2
    pair_rows = 2 * KEY_CHUNK

    @pl.when(n_chunks % 2 == 1)
    def _():
        pad_start = pl.multiple_of(n_chunks * KEY_CHUNK, KEY_CHUNK)
        keys_ref[pl.ds(pad_start, KEY_CHUNK), :] = jnp.full(
            (KEY_CHUNK, Q_TILE), INT_MIN, I32)

    pair_iota = lax.broadcasted_iota(I32, (pair_rows, Q_TILE), 0)

    def count(pred):
        def body(c, accs):
            start = pl.multiple_of(c * pair_rows, pair_rows)
            kk = keys_ref[pl.ds(start, pair_rows), :]
            hit = jnp.where(pred(kk, start), 1, 0)
            accs = list(accs)
            for r in range(pair_rows // SUBLANES):
                k = r % COUNT_ACCS
                accs[k] = accs[k] + hit[r * SUBLANES:(r + 1) * SUBLANES]
            return tuple(accs)
        zeros = jnp.zeros((SUBLANES, Q_TILE), I32)
        accs = lax.fori_loop(0, n_pairs, body, (zeros,) * COUNT_ACCS)
        total = accs[0]
        for k in range(1, COUNT_ACCS):
            total = total + accs[k]
        return jnp.sum(total, axis=0, keepdims=True)

    zero = jnp.zeros((1, Q_TILE), I32)
    cnt0 = count(lambda kk, start: kk >= zero)
    cand = jnp.where(cnt0 >= topk, zero, jnp.full((1, Q_TILE), INT_MIN, I32))

    def bit_step(i, cand):
        trial = cand | (jnp.int32(1) << (30 - i))
        cnt = count(lambda kk, start: kk >= trial)
        return jnp.where(cnt >= topk, trial, cand)

    thr = lax.fori_loop(0, 31, bit_step, cand)

    cnt_gt = count(lambda kk, start: kk > thr)
    cnt_eq = count(lambda kk, start: kk == thr)
    need = topk - cnt_gt
    real = thr != INT_MIN
    tsel_ref[...] = jnp.where(real, jnp.int32(2 ** 30), jnp.int32(-1))
    n_tied = jnp.max(jnp.where(real & (cnt_eq > need), 1, 0))

    @pl.when(n_tied > 0)
    def _():
        def pos_step(i, t):
            trial = t | (jnp.int32(1) << (pos_bits - 1 - i))
            cnt = count(lambda kk, start:
                        (kk == thr) & (start + pair_iota < trial))
            return jnp.where(cnt < need, trial, t)
        t = lax.fori_loop(0, pos_bits, pos_step, zero)
        tsel_ref[...] = jnp.where(real, t, jnp.int32(-1))

    tsel = tsel_ref[...]

    scale = ATT_HEAD_DIM ** -0.5 * LOG2_E
    for h in range(N_ATT_HEADS):
        qh = q_ref[:, h * ATT_HEAD_DIM:(h + 1) * ATT_HEAD_DIM]
        qa = _dot_nt(wuk_ref[h], qh) * scale
        qabs_ref[:, h * Q_TILE:(h + 1) * Q_TILE] = qa.astype(BF16)

    width = N_ATT_HEADS * Q_TILE
    acc_ref[...] = jnp.zeros_like(acc_ref)

    def probabilities(c, slot, m, l):
        start = pl.multiple_of(c * KEY_CHUNK, KEY_CHUNK)
        kv = ckv_ref[pl.ds(start, KEY_CHUNK), :]
        kk = keys_ref[pl.ds(start, KEY_CHUNK), :]
        pos = start + row_iota
        bias = jnp.where(
            kk > thr, 0.0,
            jnp.where(kk == thr, jnp.where(pos <= tsel, 0.0, NEG_BIG), NEG_BIG))
        m_out, l_out, alphas = [], [], []
        for pr in range(N_ATT_HEADS // 2):
            lo = 2 * pr * Q_TILE
            st = _dot(kv, qabs_ref[:, lo:lo + 2 * Q_TILE])
            for hh in range(2):
                a, b = lo + hh * Q_TILE, lo + (hh + 1) * Q_TILE
                s_h = st[:, hh * Q_TILE:(hh + 1) * Q_TILE] + bias
                m_old = m[:, a:b]
                m_new = jnp.maximum(m_old, jnp.max(s_h, axis=0, keepdims=True))
                alpha = jnp.exp2(m_old - m_new)
                p = jnp.exp2(s_h - m_new)
                l_out.append(alpha * l[:, a:b] + jnp.sum(p, axis=0, keepdims=True))
                m_out.append(m_new)
                alphas.append(alpha)
                p_ref[slot, :, a:b] = p.astype(BF16)
        cat = lambda xs: jnp.concatenate(xs, axis=1)
        return cat(m_out), cat(l_out), cat(alphas)

    def accumulate(c, slot, alpha):
        kvt = ckvt_ref[c]
        for pr in range(N_ATT_HEADS // 2):
            lo, hi = 2 * pr * Q_TILE, 2 * (pr + 1) * Q_TILE
            pv = _dot(kvt, p_ref[slot, :, lo:hi])
            acc_ref[:, lo:hi] = acc_ref[:, lo:hi] * alpha[:, lo:hi] + pv

    p_ref[1] = jnp.zeros((KEY_CHUNK, width), BF16)

    def attn_pair(i, carry):
        m, l, alpha_prev = carry
        c0 = 2 * i
        m, l, alpha0 = probabilities(c0, 0, m, l)
        accumulate(jnp.maximum(c0 - 1, 0), 1, alpha_prev)
        m, l, alpha1 = probabilities(c0 + 1, 1, m, l)
        accumulate(c0, 0, alpha0)
        return m, l, alpha1

    m0 = jnp.full((1, width), NEG_BIG, F32)
    l0 = jnp.zeros((1, width), F32)
    ones = jnp.ones((1, width), F32)
    _, l, alpha_last = lax.fori_loop(0, n_pairs, attn_pair, (m0, l0, ones))
    accumulate(2 * n_pairs - 1, 1, alpha_last)

    o_lat = (acc_ref[...] * (1.0 / l)).astype(BF16)
    for h in range(N_ATT_HEADS):
        ot = _dot(wuvt_ref[h], o_lat[:, h * Q_TILE:(h + 1) * Q_TILE])
        o_ref[:, h * ATT_HEAD_DIM:(h + 1) * ATT_HEAD_DIM] = ot.T.astype(BF16)


def _dsa_attention(proj_b, wt, ckv, ckvt, kidx, wuk, wuvt):
    b, s, _ = proj_b.shape
    topk = min(TOPK_MAX, s // 4)
    width = N_ATT_HEADS * Q_TILE
    return pl.pallas_call(
        functools.partial(_dsa_kernel, topk=topk, pos_bits=(s - 1).bit_length()),
        grid=(b, s // Q_TILE),
        in_specs=[
            pl.BlockSpec((None, Q_TILE, ATT_WIDTH),
                         lambda bi, i: (bi, i, PB_Q // ATT_WIDTH)),
            pl.BlockSpec((None, Q_TILE, IDX_HEADS * IDX_DIM),
                         lambda bi, i: (bi, i, PB_QIDX // (IDX_HEADS * IDX_DIM))),
            pl.BlockSpec((None, IDX_HEADS, Q_TILE), lambda bi, i: (bi, 0, i)),
            pl.BlockSpec((None, s, KV_RANK), lambda bi, i: (bi, 0, 0)),
            pl.BlockSpec((None, s // KEY_CHUNK, KV_RANK, KEY_CHUNK),
                         lambda bi, i: (bi, 0, 0, 0)),
            pl.BlockSpec((None, s, IDX_DIM), lambda bi, i: (bi, 0, 0)),
            pl.BlockSpec((N_ATT_HEADS, KV_RANK, ATT_HEAD_DIM),
                         lambda bi, i: (0, 0, 0)),
            pl.BlockSpec((N_ATT_HEADS, ATT_HEAD_DIM, KV_RANK),
                         lambda bi, i: (0, 0, 0)),
        ],
        out_specs=pl.BlockSpec((None, Q_TILE, ATT_WIDTH), lambda bi, i: (bi, i, 0)),
        out_shape=jax.ShapeDtypeStruct((b, s, ATT_WIDTH), BF16),
        scratch_shapes=[
            pltpu.VMEM((s + KEY_CHUNK, Q_TILE), I32),
            pltpu.VMEM((IDX_HEADS * Q_TILE, IDX_DIM), BF16),
            pltpu.VMEM((KV_RANK, width), BF16),
            pltpu.VMEM((KV_RANK, width), F32),
            pltpu.VMEM((1, Q_TILE), I32),
            pltpu.VMEM((2, KEY_CHUNK, width), BF16),
        ],
        compiler_params=_params("arbitrary", "arbitrary"),
        name="dsa_attention",
    )(proj_b, proj_b, wt, ckv, ckvt, kidx, wuk, wuvt)


def _ssd_kernel(z_ref, xs_ref, bm_ref, cm_ref, dt_ref,
                cwx_ref, cwb_ref, cwc_ref, cbx_ref, cbb_ref, cbc_ref,
                dtb_ref, alog_ref, dskip_ref, ng_ref, y_ref,
                ex_ref, eb_ref, ec_ref, state_ref, yacc_ref, *, chunk):
    ci = pl.program_id(1)
    halo = SUBLANES

    @pl.when(ci == 0)
    def _():
        ex_ref[0:halo, :] = jnp.zeros((halo, D_INNER), F32)
        eb_ref[0:halo, :] = jnp.zeros((halo, BC_WIDTH), F32)
        ec_ref[0:halo, :] = jnp.zeros((halo, BC_WIDTH), F32)
        state_ref[...] = jnp.zeros_like(state_ref)

    def conv_silu(in_ref, ext_ref, w_ref, b_ref):
        ext_ref[halo:halo + chunk, :] = in_ref[...].astype(F32)
        acc = b_ref[...]
        for k in range(CONV_WIDTH):
            off = halo - (CONV_WIDTH - 1) + k
            acc = acc + w_ref[k:k + 1, :] * ext_ref[pl.ds(off, chunk), :]
        ext_ref[0:halo, :] = ext_ref[chunk:chunk + halo, :]
        return acc * jax.nn.sigmoid(acc)

    xs = conv_silu(xs_ref, ex_ref, cwx_ref, cbx_ref)
    bm = conv_silu(bm_ref, eb_ref, cwb_ref, cbb_ref)
    cm = conv_silu(cm_ref, ec_ref, cwc_ref, cbc_ref)
    xs_b = xs.astype(BF16)

    dt = jax.nn.softplus(dt_ref[...] + dtb_ref[...])
    a = -jnp.exp(alog_ref[...])
    r_io = lax.broadcasted_iota(I32, (chunk, chunk), 0)
    c_io = lax.broadcasted_iota(I32, (chunk, chunk), 1)
    tril = r_io >= c_io
    a_cum = jnp.dot(tril.astype(F32), dt * a, precision=lax.Precision.HIGHEST,
                    preferred_element_type=F32)
    total = a_cum[chunk - 1:chunk, :]
    to_end = jnp.exp(total - a_cum) * dt
    a_cum_t = a_cum.T
    dt_t = dt.T
    to_end_t = to_end.T

    for g in range(N_SSM_GROUPS):
        bg = bm[:, g * D_STATE:(g + 1) * D_STATE]
        cg = cm[:, g * D_STATE:(g + 1) * D_STATE]
        cb = _dot_nt(cg.astype(BF16), bg.astype(BF16))
        bg_t = bg.T
        for hh in range(HEADS_PER_GROUP):
            h = g * HEADS_PER_GROUP + hh
            col = jnp.broadcast_to(a_cum[:, h:h + 1], (chunk, chunk))
            row = a_cum_t[h:h + 1, :]
            decay = jnp.exp(jnp.where(tril, col - row, -jnp.inf))
            m_h = cb * decay * dt_t[h:h + 1, :]
            if chunk == D_STATE:
                ecol = jnp.exp(col)
            else:
                ecol = jnp.exp(jnp.broadcast_to(a_cum[:, h:h + 1], (chunk, D_STATE)))
            c_h = cg * ecol
            lhs = jnp.concatenate([m_h.astype(BF16), c_h.astype(BF16)], axis=1)
            xs_h = xs_b[:, h * SSM_HEAD_DIM:(h + 1) * SSM_HEAD_DIM]
            lo, hi = h * SSM_HEAD_DIM, (h + 1) * SSM_HEAD_DIM
            prev = state_ref[:, lo:hi]
            rhs = jnp.concatenate([xs_h, prev.astype(BF16)], axis=0)
            yacc_ref[:, lo:hi] = _dot(lhs, rhs)
            new = _dot((bg_t * to_end_t[h:h + 1, :]).astype(BF16), xs_h)
            state_ref[:, lo:hi] = prev * ecol[chunk - 1:chunk, :SSM_HEAD_DIM] + new

    y = yacc_ref[...] + xs * dskip_ref[...]
    zf = z_ref[...].astype(F32)
    y = y * (zf * jax.nn.sigmoid(zf))
    gw = D_INNER // N_SSM_GROUPS
    for g in range(N_SSM_GROUPS):
        yg = y[:, g * gw:(g + 1) * gw]
        yn = yg * _rms_scale(yg) * ng_ref[:, g * gw:(g + 1) * gw]
        y_ref[:, g * gw:(g + 1) * gw] = yn.astype(BF16)


def _ssd(proj_b, proj_f, conv_w, conv_b, dt_bias, a_log, d_skip, ssm_norm_g):
    b, s, _ = proj_b.shape
    chunk = min(SSD_CHUNK, s)
    pad = LANES - N_SSM_HEADS
    cw_x, cw_b, cw_c = (conv_w[:, :D_INNER], conv_w[:, D_INNER:D_INNER + BC_WIDTH],
                        conv_w[:, D_INNER + BC_WIDTH:])
    cb = conv_b.reshape(1, -1)
    cb_x, cb_b, cb_c = (cb[:, :D_INNER], cb[:, D_INNER:D_INNER + BC_WIDTH],
                        cb[:, D_INNER + BC_WIDTH:])
    dtb = jnp.pad(dt_bias, (0, pad)).reshape(1, LANES)
    alog = jnp.pad(a_log, (0, pad)).reshape(1, LANES)
    dskip = jnp.repeat(d_skip, SSM_HEAD_DIM).reshape(1, D_INNER)
    const = lambda bi, i: (0, 0)
    return pl.pallas_call(
        functools.partial(_ssd_kernel, chunk=chunk),
        grid=(b, s // chunk),
        in_specs=[
            pl.BlockSpec((None, chunk, D_INNER), lambda bi, i: (bi, i, PB_Z // D_INNER)),
            pl.BlockSpec((None, chunk, D_INNER), lambda bi, i: (bi, i, PB_XS // D_INNER)),
            pl.BlockSpec((None, chunk, BC_WIDTH), lambda bi, i: (bi, i, PB_BM // BC_WIDTH)),
            pl.BlockSpec((None, chunk, BC_WIDTH), lambda bi, i: (bi, i, PB_CM // BC_WIDTH)),
            pl.BlockSpec((None, chunk, LANES), lambda bi, i: (bi, i, PF_DT // LANES)),
            pl.BlockSpec((CONV_WIDTH, D_INNER), const),
            pl.BlockSpec((CONV_WIDTH, BC_WIDTH), const),
            pl.BlockSpec((CONV_WIDTH, BC_WIDTH), const),
            pl.BlockSpec((1, D_INNER), const),
            pl.BlockSpec((1, BC_WIDTH), const),
            pl.BlockSpec((1, BC_WIDTH), const),
            pl.BlockSpec((1, LANES), const),
            pl.BlockSpec((1, LANES), const),
            pl.BlockSpec((1, D_INNER), const),
            pl.BlockSpec((1, D_INNER), const),
        ],
        out_specs=pl.BlockSpec((None, chunk, D_INNER), lambda bi, i: (bi, i, 0)),
        out_shape=jax.ShapeDtypeStruct((b, s, D_INNER), BF16),
        scratch_shapes=[
            pltpu.VMEM((chunk + 2 * SUBLANES, D_INNER), F32),
            pltpu.VMEM((chunk + 2 * SUBLANES, BC_WIDTH), F32),
            pltpu.VMEM((chunk + 2 * SUBLANES, BC_WIDTH), F32),
            pltpu.VMEM((D_STATE, D_INNER), F32),
            pltpu.VMEM((chunk, D_INNER), F32),
        ],
        compiler_params=_params("arbitrary", "arbitrary"),
        name="ssd_mixer",
    )(proj_b, proj_b, proj_b, proj_b, proj_f, cw_x, cw_b, cw_c, cb_x, cb_b, cb_c,
      dtb, alog, dskip, ssm_norm_g.reshape(1, D_INNER))


def _mix_out_kernel(att_ref, y_ref, ga_ref, gs_ref, x_ref, gate_ref, g_ref,
                    woa_ref, wos_ref, wout_ref, o_ref):
    ba = _dot(att_ref[...], woa_ref[...])
    bs = _dot(y_ref[...], wos_ref[...])
    merged = (jax.nn.sigmoid(ga_ref[...].astype(F32)) * ba
              + jax.nn.sigmoid(gs_ref[...].astype(F32)) * bs)
    out = _dot(merged.astype(BF16), wout_ref[...])
    normed = out * _rms_scale(out) * g_ref[...]
    o_ref[...] = x_ref[...] + gate_ref[...] * normed


def _mix_out(att, y, proj_b, x, gate, g, woa, wos, wout):
    b, s, d = x.shape
    tm = min(OUT_TM, s)
    const = lambda bi, i: (0, 0)
    return pl.pallas_call(
        _mix_out_kernel,
        grid=(b, s // tm),
        in_specs=[
            pl.BlockSpec((None, tm, ATT_WIDTH), lambda bi, i: (bi, i, 0)),
            pl.BlockSpec((None, tm, D_INNER), lambda bi, i: (bi, i, 0)),
            pl.BlockSpec((None, tm, d), lambda bi, i: (bi, i, PB_GATT // d)),
            pl.BlockSpec((None, tm, d), lambda bi, i: (bi, i, PB_GSSD // d)),
            pl.BlockSpec((None, tm, d), lambda bi, i: (bi, i, 0)),
            pl.BlockSpec((None, 1, d), lambda bi, i: (bi, 0, 0)),
            pl.BlockSpec((1, d), const),
            pl.BlockSpec((ATT_WIDTH, d), const),
            pl.BlockSpec((D_INNER, d), const),
            pl.BlockSpec((d, d), const),
        ],
        out_specs=pl.BlockSpec((None, tm, d), lambda bi, i: (bi, i, 0)),
        out_shape=jax.ShapeDtypeStruct((b, s, d), F32),
        compiler_params=_params("arbitrary", "arbitrary"),
        name="mix_out",
    )(att, y, proj_b, proj_b, x, gate, g, woa, wos, wout)


def _mlp_kernel(x_ref, g_in_ref, sc_ref, sh_ref, wup_ref, wdn_ref, gate_ref,
                g_out_ref, o_ref, h_ref, acc_ref):
    j = pl.program_id(2)

    @pl.when(j == 0)
    def _():
        x = x_ref[...]
        h = x * _rms_scale(x) * g_in_ref[...]
        h = h * (1.0 + sc_ref[...]) + sh_ref[...]
        h_ref[...] = h.astype(BF16)
        acc_ref[...] = jnp.zeros_like(acc_ref)

    up = jnp.maximum(_dot(h_ref[...], wup_ref[...]), 0.0)
    acc_ref[...] += _dot((up * up).astype(BF16), wdn_ref[...])

    @pl.when(j == pl.num_programs(2) - 1)
    def _():
        y = acc_ref[...]
        normed = y * _rms_scale(y) * g_out_ref[...]
        o_ref[...] = x_ref[...] + gate_ref[...] * normed


def _mlp(x, g_in, sc, sh, wup, wdn, gate, g_out):
    b, s, d = x.shape
    ff = wup.shape[1]
    tm = min(MLP_TM, s)
    per_b = lambda bi, i, j: (bi, 0, 0)
    const = lambda bi, i, j: (0, 0)
    return pl.pallas_call(
        _mlp_kernel,
        grid=(b, s // tm, ff // MLP_TF),
        in_specs=[
            pl.BlockSpec((None, tm, d), lambda bi, i, j: (bi, i, 0)),
            pl.BlockSpec((1, d), const),
            pl.BlockSpec((None, 1, d), per_b),
            pl.BlockSpec((None, 1, d), per_b),
            pl.BlockSpec((d, MLP_TF), lambda bi, i, j: (0, j)),
            pl.BlockSpec((MLP_TF, d), lambda bi, i, j: (j, 0)),
            pl.BlockSpec((None, 1, d), per_b),
            pl.BlockSpec((1, d), const),
        ],
        out_specs=pl.BlockSpec((None, tm, d), lambda bi, i, j: (bi, i, 0)),
        out_shape=jax.ShapeDtypeStruct((b, s, d), F32),
        scratch_shapes=[pltpu.VMEM((tm, d), BF16), pltpu.VMEM((tm, d), F32)],
        compiler_params=_params("arbitrary", "arbitrary", "arbitrary"),
        name="mlp",
    )(x, g_in, sc, sh, wup, wdn, gate, g_out)


def _split_offsets():
    sizes = (ATT_WIDTH, KV_RANK, IDX_HEADS * IDX_DIM, IDX_DIM, IDX_HEADS,
             D_INNER, D_INNER, BC_WIDTH, BC_WIDTH, N_SSM_HEADS, ATT_WIDTH, ATT_WIDTH)
    names = ("q", "kv", "qidx", "kidx", "widx", "z", "xs", "bm", "cm", "dt",
             "gatt", "gssd")
    offs = np.concatenate([[0], np.cumsum(sizes)])
    return {n: (int(offs[i]), int(offs[i + 1])) for i, n in enumerate(names)}


def _pack_w_in(w_in):
    sl = _split_offsets()
    col = lambda n: w_in[..., sl[n][0]:sl[n][1]]
    wb = jnp.concatenate([col(n) for n in
                          ("z", "xs", "q", "gatt", "gssd", "bm", "cm", "qidx")],
                         axis=-1).astype(BF16)
    lead = w_in.shape[:-1]
    zeros = lambda n: jnp.zeros(lead + (n,), w_in.dtype)
    wf = jnp.concatenate([
        col("kv"), col("kidx"), col("widx"), zeros(LANES - IDX_DIM - IDX_HEADS),
        col("dt"), zeros(LANES - N_SSM_HEADS)], axis=-1).astype(BF16)
    return wb, wf


def kernel(x, c, ada_w, ada_b, norm_g, w_in, kv_norm_g, kidx_norm_g, w_uk, w_uv,
           conv_w, conv_b, dt_bias, a_log, d_skip, ssm_norm_g, w_o_att, w_o_ssd,
           w_out, w_up, w_down):
    depth = ada_w.shape[0]
    d = x.shape[-1]
    mod = _modulation(c, ada_w, ada_b)
    wb_all, wf_all = _pack_w_in(w_in)
    wuk_all = jnp.transpose(w_uk, (0, 2, 1, 3)).astype(BF16)
    wuvt_all = jnp.transpose(w_uv, (0, 2, 3, 1)).astype(BF16)
    woa_all = w_o_att.astype(BF16)
    wos_all = w_o_ssd.astype(BF16)
    wout_all = w_out.astype(BF16)
    wup_all = w_up.astype(BF16)
    wdn_all = w_down.astype(BF16)

    for layer in range(depth):
        sh_m, sc_m, gt_m, sh_f, sc_f, gt_f = (mod[layer, :, k] for k in range(N_MOD))
        ng = lambda k: norm_g[layer, k].reshape(1, d)
        proj_b, proj_f = _input_projection(x, ng(0), sc_m, sh_m,
                                           wb_all[layer], wf_all[layer])
        ckv, ckvt, kidx, wt = _latent_prep(
            proj_f, kv_norm_g[layer].reshape(1, KV_RANK),
            kidx_norm_g[layer].reshape(1, IDX_DIM))
        att = _dsa_attention(proj_b, wt, ckv, ckvt, kidx,
                             wuk_all[layer], wuvt_all[layer])
        y = _ssd(proj_b, proj_f, conv_w[layer], conv_b[layer], dt_bias[layer],
                 a_log[layer], d_skip[layer], ssm_norm_g[layer])
        x = _mix_out(att, y, proj_b, x, gt_m, ng(1), woa_all[layer],
                     wos_all[layer], wout_all[layer])
        x = _mlp(x, ng(2), sc_f, sh_f, wup_all[layer], wdn_all[layer], gt_f, ng(3))
    return x
```

```python
import functools

import jax
import jax.numpy as jnp
import numpy as np
from jax import lax
from jax.experimental import pallas as pl
from jax.experimental.pallas import tpu as pltpu

F32 = jnp.float32
BF16 = jnp.bfloat16
I32 = jnp.int32

N_ATT_HEADS = 8
ATT_HEAD_DIM = 128
ATT_WIDTH = N_ATT_HEADS * ATT_HEAD_DIM
KV_RANK = 256
IDX_HEADS = 8
IDX_DIM = 64
TOPK_MAX = 256
D_STATE = 128
SSM_HEAD_DIM = 64
N_SSM_GROUPS = 4
HEADS_PER_GROUP = 8
N_SSM_HEADS = N_SSM_GROUPS * HEADS_PER_GROUP
D_INNER = N_SSM_HEADS * SSM_HEAD_DIM
BC_WIDTH = N_SSM_GROUPS * D_STATE
CONV_WIDTH = 4
N_MOD = 6
EPS = 1e-6

LANES = 128
SUBLANES = 8
VMEM_LIMIT_BYTES = 52 * 1024 * 1024

Q_TILE = 128
KEY_CHUNK = 256
CKVT_ROWS = KV_RANK + 16
SSD_CHUNK = 128
INPROJ_TM = 2048
INPROJ_TN = 512
OUT_TM = 512
MLP_TM = 1024
MLP_TF = 1024
MOD_TN = 1536

INT_MIN = -2147483648
LOG2_E = 1.4426950408889634
COUNT_ACCS = 8
NEG_BIG = -1e30

PB_Z, PB_XS, PB_Q, PB_GATT, PB_GSSD, PB_BM, PB_CM, PB_QIDX = (
    0, 2048, 4096, 5120, 6144, 7168, 7680, 8192)
PB_WIDTH = 8704
PF_KV, PF_KIDX, PF_DT = 0, 256, 384
PF_WIDTH = 512


def _params(*sem):
    return pltpu.CompilerParams(dimension_semantics=sem,
                                vmem_limit_bytes=VMEM_LIMIT_BYTES)


def _dot(a, b):
    return jnp.dot(a, b, preferred_element_type=F32)


def _dot_nt(a, b):
    return lax.dot_general(a, b, (((1,), (1,)), ((), ())),
                           preferred_element_type=F32)


def _rms_scale(x):
    return lax.rsqrt(jnp.mean(x * x, axis=-1, keepdims=True) + EPS)


def _mod_kernel(c_ref, w_ref, b_ref, o_ref):
    c = c_ref[...]
    c_act = (c * jax.nn.sigmoid(c)).astype(BF16)
    o_ref[...] = _dot(c_act, w_ref[...].astype(BF16)) + b_ref[...]


def _modulation(c, ada_w, ada_b):
    n_layers, d, n = ada_w.shape
    b = c.shape[0]
    rows = ((b + SUBLANES - 1) // SUBLANES) * SUBLANES
    c_pad = jnp.zeros((rows, d), F32).at[:b].set(c)
    out = pl.pallas_call(
        _mod_kernel,
        grid=(n_layers, n // MOD_TN),
        in_specs=[
            pl.BlockSpec((rows, d), lambda l, j: (0, 0)),
            pl.BlockSpec((None, d, MOD_TN), lambda l, j: (l, 0, j)),
            pl.BlockSpec((None, 1, MOD_TN), lambda l, j: (l, 0, j)),
        ],
        out_specs=pl.BlockSpec((None, rows, MOD_TN), lambda l, j: (l, 0, j)),
        out_shape=jax.ShapeDtypeStruct((n_layers, rows, n), F32),
        compiler_params=_params("arbitrary", "arbitrary"),
        name="adaln_mod",
    )(c_pad, ada_w, ada_b.reshape(n_layers, 1, n))
    return out[:, :b].reshape(n_layers, b, N_MOD, 1, d)


def _inproj_kernel(x_ref, g_ref, sc_ref, sh_ref, wb_ref, wf_ref,
                   ob_ref, of_ref, h_ref, *, n_bf16_tiles):
    j = pl.program_id(2)

    @pl.when(j == 0)
    def _():
        x = x_ref[...]
        h = x * _rms_scale(x) * g_ref[...]
        h = h * (1.0 + sc_ref[...]) + sh_ref[...]
        h_ref[...] = h.astype(BF16)

    @pl.when(j < n_bf16_tiles)
    def _():
        ob_ref[...] = _dot(h_ref[...], wb_ref[...]).astype(BF16)

    @pl.when(j == n_bf16_tiles)
    def _():
        of_ref[...] = _dot(h_ref[...], wf_ref[...])


def _input_projection(x, g, sc, sh, wb, wf):
    b, s, d = x.shape
    tm = min(INPROJ_TM, s)
    nj = PB_WIDTH // INPROJ_TN
    last = nj - 1
    return pl.pallas_call(
        functools.partial(_inproj_kernel, n_bf16_tiles=nj),
        grid=(b, s // tm, nj + 1),
        in_specs=[
            pl.BlockSpec((None, tm, d), lambda bi, i, j: (bi, i, 0)),
            pl.BlockSpec((1, d), lambda bi, i, j: (0, 0)),
            pl.BlockSpec((None, 1, d), lambda bi, i, j: (bi, 0, 0)),
            pl.BlockSpec((None, 1, d), lambda bi, i, j: (bi, 0, 0)),
            pl.BlockSpec((d, INPROJ_TN),
                         lambda bi, i, j: (0, jnp.minimum(j, last))),
            pl.BlockSpec((d, PF_WIDTH), lambda bi, i, j: (0, 0)),
        ],
        out_specs=[
            pl.BlockSpec((None, tm, INPROJ_TN),
                         lambda bi, i, j: (bi, i, jnp.minimum(j, last))),
            pl.BlockSpec((None, tm, PF_WIDTH), lambda bi, i, j: (bi, i, 0)),
        ],
        out_shape=[
            jax.ShapeDtypeStruct((b, s, PB_WIDTH), BF16),
            jax.ShapeDtypeStruct((b, s, PF_WIDTH), F32),
        ],
        scratch_shapes=[pltpu.VMEM((tm, d), BF16)],
        compiler_params=_params("arbitrary", "arbitrary", "arbitrary"),
        name="norm_inproj",
    )(x, g, sc, sh, wb, wf)


def _prep_kernel(kv_ref, kw_ref, gkv_ref, gk_ref, ckv_ref, ckvt_ref,
                 kidx_ref, wt_ref, *, n_chunks):
    kv = kv_ref[...]
    ckv = kv * _rms_scale(kv) * gkv_ref[...]
    ckv_ref[...] = ckv.astype(BF16)
    extra_row = lax.broadcasted_iota(I32, (CKVT_ROWS - KV_RANK, KEY_CHUNK), 0)
    extra = jnp.where(extra_row == 0, 1.0, 0.0).astype(BF16)
    for c in range(n_chunks):
        blk = ckv[c * KEY_CHUNK:(c + 1) * KEY_CHUNK, :]
        ckvt_ref[c] = jnp.concatenate([blk.T.astype(BF16), extra], axis=0)
    kw = kw_ref[...]
    k = kw[:, :IDX_DIM]
    kn = k * _rms_scale(k) * gk_ref[...]
    kidx_ref[...] = kn.astype(BF16)
    kwt = kw.T
    wt_ref[...] = kwt[IDX_DIM:IDX_DIM + IDX_HEADS, :] * (IDX_HEADS ** -0.5)


def _latent_prep(proj_f, kv_norm_g, kidx_norm_g):
    b, s, _ = proj_f.shape
    ts = min(1024, s)
    n_chunks = ts // KEY_CHUNK
    return pl.pallas_call(
        functools.partial(_prep_kernel, n_chunks=n_chunks),
        grid=(b, s // ts),
        in_specs=[
            pl.BlockSpec((None, ts, KV_RANK),
                         lambda bi, i: (bi, i, PF_KV // KV_RANK)),
            pl.BlockSpec((None, ts, LANES),
                         lambda bi, i: (bi, i, PF_KIDX // LANES)),
            pl.BlockSpec((1, KV_RANK), lambda bi, i: (0, 0)),
            pl.BlockSpec((1, IDX_DIM), lambda bi, i: (0, 0)),
        ],
        out_specs=[
            pl.BlockSpec((None, ts, KV_RANK), lambda bi, i: (bi, i, 0)),
            pl.BlockSpec((None, n_chunks, CKVT_ROWS, KEY_CHUNK),
                         lambda bi, i: (bi, i, 0, 0)),
            pl.BlockSpec((None, ts, IDX_DIM), lambda bi, i: (bi, i, 0)),
            pl.BlockSpec((None, IDX_HEADS, ts), lambda bi, i: (bi, 0, i)),
        ],
        out_shape=[
            jax.ShapeDtypeStruct((b, s, KV_RANK), BF16),
            jax.ShapeDtypeStruct((b, s // KEY_CHUNK, CKVT_ROWS, KEY_CHUNK), BF16),
            jax.ShapeDtypeStruct((b, s, IDX_DIM), BF16),
            jax.ShapeDtypeStruct((b, IDX_HEADS, s), F32),
        ],
        compiler_params=_params("arbitrary", "arbitrary"),
        name="latent_prep",
    )(proj_f, proj_f, kv_norm_g, kidx_norm_g)


def _dsa_kernel(q_ref, qidx_ref, wt_ref, ckv_ref, ckvt_ref, kidx_ref,
                wuk_ref, wuvt_ref, o_ref,
                keys_ref, qall_ref, qabs_ref, acc_ref, tsel_ref, p_ref,
                *, topk, pos_bits):
    qb = pl.program_id(1)
    n_chunks = (qb * Q_TILE + Q_TILE + KEY_CHUNK - 1) // KEY_CHUNK
    q_pos = qb * Q_TILE + lax.broadcasted_iota(I32, (KEY_CHUNK, Q_TILE), 1)
    row_iota = lax.broadcasted_iota(I32, (KEY_CHUNK, Q_TILE), 0)

    for h in range(IDX_HEADS):
        qall_ref[h * Q_TILE:(h + 1) * Q_TILE, :] = (
            qidx_ref[:, h * IDX_DIM:(h + 1) * IDX_DIM])

    n_pairs = (n_chunks + 1) // 2
    pair_rows = 2 * KEY_CHUNK

    def score_chunk(c):
        start = pl.multiple_of(c * KEY_CHUNK, KEY_CHUNK)
        ks = kidx_ref[pl.ds(start, KEY_CHUNK), :]
        logits = _dot_nt(ks, qall_ref[...])
        score = jnp.zeros((KEY_CHUNK, Q_TILE), F32)
        for h in range(IDX_HEADS):
            lg = logits[:, h * Q_TILE:(h + 1) * Q_TILE]
            score = score + wt_ref[h:h + 1, :] * jnp.maximum(lg, 0.0)
        bits = pltpu.bitcast(score, I32)
        key = bits ^ ((bits >> 31) & 0x7FFFFFFF)
        key = jnp.where(start + row_iota > q_pos, INT_MIN, key)
        keys_ref[pl.ds(start, KEY_CHUNK), :] = key

    def score_pair(i, carry):
        score_chunk(2 * i)
        score_chunk(2 * i + 1)
        return carry

    lax.fori_loop(0, n_pairs, score_pair, 0)

    pair_iota = lax.broadcasted_iota(I32, (pair_rows, Q_TILE), 0)

    def count(pred):
        def body(c, accs):
            start = pl.multiple_of(c * pair_rows, pair_rows)
            kk = keys_ref[pl.ds(start, pair_rows), :]
            hit = jnp.where(pred(kk, start), 1, 0)
            accs = list(accs)
            for r in range(pair_rows // SUBLANES):
                k = r % COUNT_ACCS
                accs[k] = accs[k] + hit[r * SUBLANES:(r + 1) * SUBLANES]
            return tuple(accs)
        zeros = jnp.zeros((SUBLANES, Q_TILE), I32)
        accs = lax.fori_loop(0, n_pairs, body, (zeros,) * COUNT_ACCS)
        total = accs[0]
        for k in range(1, COUNT_ACCS):
            total = total + accs[k]
        return jnp.sum(total, axis=0, keepdims=True)

    zero = jnp.zeros((1, Q_TILE), I32)
    cnt0 = count(lambda kk, start: kk >= zero)
    cand = jnp.where(cnt0 >= topk, zero, jnp.full((1, Q_TILE), INT_MIN, I32))

    def bit_step(i, carry):
        cand, cand_cnt = carry
        trial = cand | (jnp.int32(1) << (30 - i))
        cnt = count(lambda kk, start: kk >= trial)
        keep = cnt >= topk
        return jnp.where(keep, trial, cand), jnp.where(keep, cnt, cand_cnt)

    thr, thr_cnt = lax.fori_loop(0, 31, bit_step, (cand, cnt0))

    real = thr != INT_MIN
    tsel_ref[...] = jnp.where(real, jnp.int32(2 ** 30), jnp.int32(-1))
    n_tied = jnp.max(jnp.where(real & (thr_cnt > topk), 1, 0))

    @pl.when(n_tied > 0)
    def _():
        need = topk - count(lambda kk, start: kk > thr)

        def pos_step(i, t):
            trial = t | (jnp.int32(1) << (pos_bits - 1 - i))
            cnt = count(lambda kk, start:
                        (kk == thr) & (start + pair_iota < trial))
            return jnp.where(cnt < need, trial, t)
        t = lax.fori_loop(0, pos_bits, pos_step, zero)
        tsel_ref[...] = jnp.where(real, t, jnp.int32(-1))

    tsel = tsel_ref[...]

    scale = ATT_HEAD_DIM ** -0.5 * LOG2_E
    for h in range(N_ATT_HEADS):
        qh = q_ref[:, h * ATT_HEAD_DIM:(h + 1) * ATT_HEAD_DIM]
        qa = _dot_nt(wuk_ref[h], qh) * scale
        qabs_ref[:, h * Q_TILE:(h + 1) * Q_TILE] = qa.astype(BF16)

    width = N_ATT_HEADS * Q_TILE
    acc_ref[...] = jnp.zeros_like(acc_ref)

    def probabilities(c, slot, m):
        start = pl.multiple_of(c * KEY_CHUNK, KEY_CHUNK)
        kv = ckv_ref[pl.ds(start, KEY_CHUNK), :]
        kk = keys_ref[pl.ds(start, KEY_CHUNK), :]
        pos = start + row_iota
        bias = jnp.where(
            kk > thr, 0.0,
            jnp.where(kk == thr, jnp.where(pos <= tsel, 0.0, NEG_BIG), NEG_BIG))
        m_out, alphas = [], []
        for pr in range(N_ATT_HEADS // 2):
            lo = 2 * pr * Q_TILE
            st = _dot(kv, qabs_ref[:, lo:lo + 2 * Q_TILE])
            for hh in range(2):
                a, b = lo + hh * Q_TILE, lo + (hh + 1) * Q_TILE
                s_h = st[:, hh * Q_TILE:(hh + 1) * Q_TILE] + bias
                m_old = m[:, a:b]
                m_new = jnp.maximum(m_old, jnp.max(s_h, axis=0, keepdims=True))
                alpha = jnp.exp2(m_old - m_new)
                p = jnp.exp2(s_h - m_new)
                m_out.append(m_new)
                alphas.append(alpha)
                p_ref[slot, :, a:b] = p.astype(BF16)
        cat = lambda xs: jnp.concatenate(xs, axis=1)
        return cat(m_out), cat(alphas)

    def accumulate(c, slot, alpha):
        kvt = ckvt_ref[c]
        for pr in range(N_ATT_HEADS // 2):
            lo, hi = 2 * pr * Q_TILE, 2 * (pr + 1) * Q_TILE
            pv = _dot(kvt, p_ref[slot, :, lo:hi])
            acc_ref[:, lo:hi] = acc_ref[:, lo:hi] * alpha[:, lo:hi] + pv

    p_ref[1] = jnp.zeros((KEY_CHUNK, width), BF16)

    def attn_pair(i, carry):
        m, alpha_prev = carry
        c0 = 2 * i
        m, alpha0 = probabilities(c0, 0, m)
        accumulate(jnp.maximum(c0 - 1, 0), 1, alpha_prev)
        m, alpha1 = probabilities(c0 + 1, 1, m)
        accumulate(c0, 0, alpha0)
        return m, alpha1

    m0 = jnp.full((1, width), NEG_BIG, F32)
    ones = jnp.ones((1, width), F32)
    _, alpha_last = lax.fori_loop(0, n_pairs, attn_pair, (m0, ones))
    accumulate(2 * n_pairs - 1, 1, alpha_last)

    denom = acc_ref[KV_RANK:KV_RANK + 1, :]
    o_lat = (acc_ref[0:KV_RANK, :] * (1.0 / denom)).astype(BF16)
    for h in range(N_ATT_HEADS):
        ot = _dot(wuvt_ref[h], o_lat[:, h * Q_TILE:(h + 1) * Q_TILE])
        o_ref[:, h * ATT_HEAD_DIM:(h + 1) * ATT_HEAD_DIM] = ot.T.astype(BF16)


def _dsa_attention(proj_b, wt, ckv, ckvt, kidx, wuk, wuvt):
    b, s, _ = proj_b.shape
    topk = min(TOPK_MAX, s // 4)
    assert s % (2 * KEY_CHUNK) == 0
    width = N_ATT_HEADS * Q_TILE
    return pl.pallas_call(
        functools.partial(_dsa_kernel, topk=topk, pos_bits=(s - 1).bit_length()),
        grid=(b, s // Q_TILE),
        in_specs=[
            pl.BlockSpec((None, Q_TILE, ATT_WIDTH),
                         lambda bi, i: (bi, i, PB_Q // ATT_WIDTH)),
            pl.BlockSpec((None, Q_TILE, IDX_HEADS * IDX_DIM),
                         lambda bi, i: (bi, i, PB_QIDX // (IDX_HEADS * IDX_DIM))),
            pl.BlockSpec((None, IDX_HEADS, Q_TILE), lambda bi, i: (bi, 0, i)),
            pl.BlockSpec((None, s, KV_RANK), lambda bi, i: (bi, 0, 0)),
            pl.BlockSpec((None, s // KEY_CHUNK, CKVT_ROWS, KEY_CHUNK),
                         lambda bi, i: (bi, 0, 0, 0)),
            pl.BlockSpec((None, s, IDX_DIM), lambda bi, i: (bi, 0, 0)),
            pl.BlockSpec((N_ATT_HEADS, KV_RANK, ATT_HEAD_DIM),
                         lambda bi, i: (0, 0, 0)),
            pl.BlockSpec((N_ATT_HEADS, ATT_HEAD_DIM, KV_RANK),
                         lambda bi, i: (0, 0, 0)),
        ],
        out_specs=pl.BlockSpec((None, Q_TILE, ATT_WIDTH), lambda bi, i: (bi, i, 0)),
        out_shape=jax.ShapeDtypeStruct((b, s, ATT_WIDTH), BF16),
        scratch_shapes=[
            pltpu.VMEM((s, Q_TILE), I32),
            pltpu.VMEM((IDX_HEADS * Q_TILE, IDX_DIM), BF16),
            pltpu.VMEM((KV_RANK, width), BF16),
            pltpu.VMEM((CKVT_ROWS, width), F32),
            pltpu.VMEM((1, Q_TILE), I32),
            pltpu.VMEM((2, KEY_CHUNK, width), BF16),
        ],
        compiler_params=_params("arbitrary", "arbitrary"),
        name="dsa_attention",
    )(proj_b, proj_b, wt, ckv, ckvt, kidx, wuk, wuvt)


def _ssd_kernel(z_ref, xs_ref, bm_ref, cm_ref, dt_ref, shift_ref,
                cwx_ref, cwb_ref, cwc_ref, cbx_ref, cbb_ref, cbc_ref,
                dtb_ref, alog_ref, dskip_ref, ng_ref, y_ref,
                tx_ref, tb_ref, tc_ref, state_ref, yacc_ref, *, chunk):
    ci = pl.program_id(1)
    taps = CONV_WIDTH - 1

    @pl.when(ci == 0)
    def _():
        tx_ref[...] = jnp.zeros_like(tx_ref)
        tb_ref[...] = jnp.zeros_like(tb_ref)
        tc_ref[...] = jnp.zeros_like(tc_ref)
        state_ref[...] = jnp.zeros_like(state_ref)

    def conv_silu(in_ref, tail_ref, w_ref, b_ref):
        cur = in_ref[...]
        shifted = _dot(shift_ref[...], cur)
        cur_f = cur.astype(F32)
        acc = b_ref[...] + w_ref[taps:taps + 1, :] * cur_f
        for k in range(taps):
            acc = acc + w_ref[k:k + 1, :] * shifted[k * chunk:(k + 1) * chunk]
        tail = tail_ref[...]
        row = lax.broadcasted_iota(I32, tail.shape, 0)
        fix = jnp.zeros_like(tail)
        for k in range(taps):
            back = taps - k
            fix = fix + jnp.where(row < back,
                                  w_ref[k:k + 1, :] * pltpu.roll(tail, back, 0), 0.0)
        acc = jnp.concatenate([acc[:SUBLANES] + fix, acc[SUBLANES:]], axis=0)
        tail_ref[...] = cur_f[chunk - SUBLANES:]
        return acc * jax.nn.sigmoid(acc)

    xs = conv_silu(xs_ref, tx_ref, cwx_ref, cbx_ref)
    bm = conv_silu(bm_ref, tb_ref, cwb_ref, cbb_ref)
    cm = conv_silu(cm_ref, tc_ref, cwc_ref, cbc_ref)
    xs_b = xs.astype(BF16)

    dt = jax.nn.softplus(dt_ref[...] + dtb_ref[...])
    a2 = -jnp.exp(alog_ref[...]) * LOG2_E
    r_io = lax.broadcasted_iota(I32, (chunk, chunk), 0)
    c_io = lax.broadcasted_iota(I32, (chunk, chunk), 1)
    tril = r_io >= c_io
    a_cum = jnp.dot(tril.astype(F32), dt * a2, precision=lax.Precision.HIGHEST,
                    preferred_element_type=F32)
    total = a_cum[chunk - 1:chunk, :]
    to_end_t = (jnp.exp2(total - a_cum) * dt).T
    src_t = (a_cum - jnp.log2(dt)).T
    first = lax.broadcasted_iota(I32, (1, LANES), 1) < SSM_HEAD_DIM

    for g in range(N_SSM_GROUPS):
        bg = bm[:, g * D_STATE:(g + 1) * D_STATE]
        cg = cm[:, g * D_STATE:(g + 1) * D_STATE]
        cb = _dot_nt(cg.astype(BF16), bg.astype(BF16))
        bg_t = bg.T
        for pair in range(HEADS_PER_GROUP // 2):
            h0 = g * HEADS_PER_GROUP + 2 * pair
            lo = h0 * SSM_HEAD_DIM
            lhs, bgw, e_last = [], [], []
            for h in (h0, h0 + 1):
                col = jnp.broadcast_to(a_cum[:, h:h + 1], (chunk, chunk))
                decay = jnp.exp2(jnp.where(tril, col - src_t[h:h + 1, :], -jnp.inf))
                ecol = jnp.exp2(col)
                lhs.append(jnp.concatenate(
                    [(cb * decay).astype(BF16), (cg * ecol).astype(BF16)], axis=1))
                bgw.append((bg_t * to_end_t[h:h + 1, :]).astype(BF16))
                e_last.append(ecol[chunk - 1:chunk, :])
            xs_pair = xs_b[:, lo:lo + LANES]
            prev = state_ref[:, lo:lo + LANES]
            rhs = jnp.concatenate([xs_pair, prev.astype(BF16)], axis=0)
            r = _dot(jnp.concatenate(lhs, axis=0), rhs)
            yacc_ref[:, lo:lo + LANES] = jnp.where(first, r[:chunk], r[chunk:])
            nw = _dot(jnp.concatenate(bgw, axis=0), xs_pair)
            keep = jnp.where(first, e_last[0], e_last[1])
            state_ref[:, lo:lo + LANES] = prev * keep + jnp.where(
                first, nw[:D_STATE], nw[D_STATE:])

    y = yacc_ref[...] + xs * dskip_ref[...]
    zf = z_ref[...].astype(F32)
    y = y * (zf * jax.nn.sigmoid(zf))
    gw = D_INNER // N_SSM_GROUPS
    for g in range(N_SSM_GROUPS):
        yg = y[:, g * gw:(g + 1) * gw]
        yn = yg * _rms_scale(yg) * ng_ref[:, g * gw:(g + 1) * gw]
        y_ref[:, g * gw:(g + 1) * gw] = yn.astype(BF16)


def _ssd(proj_b, proj_f, conv_w, conv_b, dt_bias, a_log, d_skip, ssm_norm_g):
    b, s, _ = proj_b.shape
    chunk = SSD_CHUNK
    assert chunk == D_STATE == LANES and s % chunk == 0
    pad = LANES - N_SSM_HEADS
    taps = CONV_WIDTH - 1
    t_io = np.arange(chunk)
    shift = np.concatenate(
        [(t_io[:, None] - (taps - k) == t_io[None, :]) for k in range(taps)], axis=0)
    shift = jnp.asarray(shift, BF16)
    cw_x, cw_b, cw_c = (conv_w[:, :D_INNER], conv_w[:, D_INNER:D_INNER + BC_WIDTH],
                        conv_w[:, D_INNER + BC_WIDTH:])
    cb = conv_b.reshape(1, -1)
    cb_x, cb_b, cb_c = (cb[:, :D_INNER], cb[:, D_INNER:D_INNER + BC_WIDTH],
                        cb[:, D_INNER + BC_WIDTH:])
    dtb = jnp.pad(dt_bias, (0, pad)).reshape(1, LANES)
    alog = jnp.pad(a_log, (0, pad)).reshape(1, LANES)
    dskip = jnp.repeat(d_skip, SSM_HEAD_DIM).reshape(1, D_INNER)
    const = lambda bi, i: (0, 0)
    return pl.pallas_call(
        functools.partial(_ssd_kernel, chunk=chunk),
        grid=(b, s // chunk),
        in_specs=[
            pl.BlockSpec((None, chunk, D_INNER), lambda bi, i: (bi, i, PB_Z // D_INNER)),
            pl.BlockSpec((None, chunk, D_INNER), lambda bi, i: (bi, i, PB_XS // D_INNER)),
            pl.BlockSpec((None, chunk, BC_WIDTH), lambda bi, i: (bi, i, PB_BM // BC_WIDTH)),
            pl.BlockSpec((None, chunk, BC_WIDTH), lambda bi, i: (bi, i, PB_CM // BC_WIDTH)),
            pl.BlockSpec((None, chunk, LANES), lambda bi, i: (bi, i, PF_DT // LANES)),
            pl.BlockSpec((taps * chunk, chunk), const),
            pl.BlockSpec((CONV_WIDTH, D_INNER), const),
            pl.BlockSpec((CONV_WIDTH, BC_WIDTH), const),
            pl.BlockSpec((CONV_WIDTH, BC_WIDTH), const),
            pl.BlockSpec((1, D_INNER), const),
            pl.BlockSpec((1, BC_WIDTH), const),
            pl.BlockSpec((1, BC_WIDTH), const),
            pl.BlockSpec((1, LANES), const),
            pl.BlockSpec((1, LANES), const),
            pl.BlockSpec((1, D_INNER), const),
            pl.BlockSpec((1, D_INNER), const),
        ],
        out_specs=pl.BlockSpec((None, chunk, D_INNER), lambda bi, i: (bi, i, 0)),
        out_shape=jax.ShapeDtypeStruct((b, s, D_INNER), BF16),
        scratch_shapes=[
            pltpu.VMEM((SUBLANES, D_INNER), F32),
            pltpu.VMEM((SUBLANES, BC_WIDTH), F32),
            pltpu.VMEM((SUBLANES, BC_WIDTH), F32),
            pltpu.VMEM((D_STATE, D_INNER), F32),
            pltpu.VMEM((chunk, D_INNER), F32),
        ],
        compiler_params=_params("arbitrary", "arbitrary"),
        name="ssd_mixer",
    )(proj_b, proj_b, proj_b, proj_b, proj_f, shift, cw_x, cw_b, cw_c, cb_x, cb_b, cb_c,
      dtb, alog, dskip, ssm_norm_g.reshape(1, D_INNER))


def _mix_out_kernel(att_ref, y_ref, ga_ref, gs_ref, x_ref, gate_ref, g_ref,
                    woa_ref, wos_ref, wout_ref, o_ref):
    ba = _dot(att_ref[...], woa_ref[...])
    bs = _dot(y_ref[...], wos_ref[...])
    merged = (jax.nn.sigmoid(ga_ref[...].astype(F32)) * ba
              + jax.nn.sigmoid(gs_ref[...].astype(F32)) * bs)
    out = _dot(merged.astype(BF16), wout_ref[...])
    normed = out * _rms_scale(out) * g_ref[...]
    o_ref[...] = x_ref[...] + gate_ref[...] * normed


def _mix_out(att, y, proj_b, x, gate, g, woa, wos, wout):
    b, s, d = x.shape
    tm = min(OUT_TM, s)
    const = lambda bi, i: (0, 0)
    return pl.pallas_call(
        _mix_out_kernel,
        grid=(b, s // tm),
        in_specs=[
            pl.BlockSpec((None, tm, ATT_WIDTH), lambda bi, i: (bi, i, 0)),
            pl.BlockSpec((None, tm, D_INNER), lambda bi, i: (bi, i, 0)),
            pl.BlockSpec((None, tm, d), lambda bi, i: (bi, i, PB_GATT // d)),
            pl.BlockSpec((None, tm, d), lambda bi, i: (bi, i, PB_GSSD // d)),
            pl.BlockSpec((None, tm, d), lambda bi, i: (bi, i, 0)),
            pl.BlockSpec((None, 1, d), lambda bi, i: (bi, 0, 0)),
            pl.BlockSpec((1, d), const),
            pl.BlockSpec((ATT_WIDTH, d), const),
            pl.BlockSpec((D_INNER, d), const),
            pl.BlockSpec((d, d), const),
        ],
        out_specs=pl.BlockSpec((None, tm, d), lambda bi, i: (bi, i, 0)),
        out_shape=jax.ShapeDtypeStruct((b, s, d), F32),
        compiler_params=_params("arbitrary", "arbitrary"),
        name="mix_out",
    )(att, y, proj_b, proj_b, x, gate, g, woa, wos, wout)


def _mlp_kernel(x_ref, g_in_ref, sc_ref, sh_ref, wup_ref, wdn_ref, gate_ref,
                g_out_ref, o_ref, h_ref, acc_ref):
    j = pl.program_id(2)

    @pl.when(j == 0)
    def _():
        x = x_ref[...]
        h = x * _rms_scale(x) * g_in_ref[...]
        h = h * (1.0 + sc_ref[...]) + sh_ref[...]
        h_ref[...] = h.astype(BF16)
        acc_ref[...] = jnp.zeros_like(acc_ref)

    up = jnp.maximum(_dot(h_ref[...], wup_ref[...]), 0.0)
    acc_ref[...] += _dot((up * up).astype(BF16), wdn_ref[...])

    @pl.when(j == pl.num_programs(2) - 1)
    def _():
        y = acc_ref[...]
        normed = y * _rms_scale(y) * g_out_ref[...]
        o_ref[...] = x_ref[...] + gate_ref[...] * normed


def _mlp(x, g_in, sc, sh, wup, wdn, gate, g_out):
    b, s, d = x.shape
    ff = wup.shape[1]
    tm = min(MLP_TM, s)
    per_b = lambda bi, i, j: (bi, 0, 0)
    const = lambda bi, i, j: (0, 0)
    return pl.pallas_call(
        _mlp_kernel,
        grid=(b, s // tm, ff // MLP_TF),
        in_specs=[
            pl.BlockSpec((None, tm, d), lambda bi, i, j: (bi, i, 0)),
            pl.BlockSpec((1, d), const),
            pl.BlockSpec((None, 1, d), per_b),
            pl.BlockSpec((None, 1, d), per_b),
            pl.BlockSpec((d, MLP_TF), lambda bi, i, j: (0, j)),
            pl.BlockSpec((MLP_TF, d), lambda bi, i, j: (j, 0)),
            pl.BlockSpec((None, 1, d), per_b),
            pl.BlockSpec((1, d), const),
        ],
        out_specs=pl.BlockSpec((None, tm, d), lambda bi, i, j: (bi, i, 0)),
        out_shape=jax.ShapeDtypeStruct((b, s, d), F32),
        scratch_shapes=[pltpu.VMEM((tm, d), BF16), pltpu.VMEM((tm, d), F32)],
        compiler_params=_params("arbitrary", "arbitrary", "arbitrary"),
        name="mlp",
    )(x, g_in, sc, sh, wup, wdn, gate, g_out)


def _split_offsets():
    sizes = (ATT_WIDTH, KV_RANK, IDX_HEADS * IDX_DIM, IDX_DIM, IDX_HEADS,
             D_INNER, D_INNER, BC_WIDTH, BC_WIDTH, N_SSM_HEADS, ATT_WIDTH, ATT_WIDTH)
    names = ("q", "kv", "qidx", "kidx", "widx", "z", "xs", "bm", "cm", "dt",
             "gatt", "gssd")
    offs = np.concatenate([[0], np.cumsum(sizes)])
    return {n: (int(offs[i]), int(offs[i + 1])) for i, n in enumerate(names)}


def _pack_w_in(w_in):
    sl = _split_offsets()
    col = lambda n: w_in[..., sl[n][0]:sl[n][1]]
    wb = jnp.concatenate([col(n) for n in
                          ("z", "xs", "q", "gatt", "gssd", "bm", "cm", "qidx")],
                         axis=-1).astype(BF16)
    lead = w_in.shape[:-1]
    zeros = lambda n: jnp.zeros(lead + (n,), w_in.dtype)
    wf = jnp.concatenate([
        col("kv"), col("kidx"), col("widx"), zeros(LANES - IDX_DIM - IDX_HEADS),
        col("dt"), zeros(LANES - N_SSM_HEADS)], axis=-1).astype(BF16)
    return wb, wf


def kernel(x, c, ada_w, ada_b, norm_g, w_in, kv_norm_g, kidx_norm_g, w_uk, w_uv,
           conv_w, conv_b, dt_bias, a_log, d_skip, ssm_norm_g, w_o_att, w_o_ssd,
           w_out, w_up, w_down):
    depth = ada_w.shape[0]
    d = x.shape[-1]
    mod = _modulation(c, ada_w, ada_b)
    wb_all, wf_all = _pack_w_in(w_in)
    wuk_all = jnp.transpose(w_uk, (0, 2, 1, 3)).astype(BF16)
    wuvt_all = jnp.transpose(w_uv, (0, 2, 3, 1)).astype(BF16)
    woa_all = w_o_att.astype(BF16)
    wos_all = w_o_ssd.astype(BF16)
    wout_all = w_out.astype(BF16)
    wup_all = w_up.astype(BF16)
    wdn_all = w_down.astype(BF16)

    for layer in range(depth):
        sh_m, sc_m, gt_m, sh_f, sc_f, gt_f = (mod[layer, :, k] for k in range(N_MOD))
        ng = lambda k: norm_g[layer, k].reshape(1, d)
        proj_b, proj_f = _input_projection(x, ng(0), sc_m, sh_m,
                                           wb_all[layer], wf_all[layer])
        ckv, ckvt, kidx, wt = _latent_prep(
            proj_f, kv_norm_g[layer].reshape(1, KV_RANK),
            kidx_norm_g[layer].reshape(1, IDX_DIM))
        att = _dsa_attention(proj_b, wt, ckv, ckvt, kidx,
                             wuk_all[layer], wuvt_all[layer])
        y = _ssd(proj_b, proj_f, conv_w[layer], conv_b[layer], dt_bias[layer],
                 a_log[layer], d_skip[layer], ssm_norm_g[layer])
        x = _mix_out(att, y, proj_b, x, gt_m, ng(1), woa_all[layer],
                     wos_all[layer], wout_all[layer])
        x = _mlp(x, ng(2), sc_f, sh_f, wup_all[layer], wdn_all[layer], gt_f, ng(3))
    return x
```

```python
import functools

import jax
import jax.numpy as jnp
import numpy as np
from jax import lax
from jax.experimental import pallas as pl
from jax.experimental.pallas import tpu as pltpu

F32 = jnp.float32
BF16 = jnp.bfloat16
I32 = jnp.int32
I16 = jnp.int16

N_ATT_HEADS = 8
ATT_HEAD_DIM = 128
ATT_WIDTH = N_ATT_HEADS * ATT_HEAD_DIM
KV_RANK = 256
IDX_HEADS = 8
IDX_DIM = 64
TOPK_MAX = 256
D_STATE = 128
SSM_HEAD_DIM = 64
N_SSM_GROUPS = 4
HEADS_PER_GROUP = 8
N_SSM_HEADS = N_SSM_GROUPS * HEADS_PER_GROUP
D_INNER = N_SSM_HEADS * SSM_HEAD_DIM
BC_WIDTH = N_SSM_GROUPS * D_STATE
CONV_WIDTH = 4
N_MOD = 6
EPS = 1e-6

LANES = 128
SUBLANES = 8
VMEM_LIMIT_BYTES = 52 * 1024 * 1024

MXU_WIDTH = 256
Q_TILE = 256
KEY_CHUNK = 256
CKVT_ROWS = KV_RANK + 16
SSD_CHUNK = 128
INPROJ_TM = 2048
INPROJ_TN = 512
OUT_TM = 512
MLP_TM = 1024
MLP_TF = 1024
MOD_TN = 1536

INT_MIN = -2147483648
I16_MIN = -32768
HALF_BIAS = 32768
PACK_ROWS = 16
LOG2_E = 1.4426950408889634
COUNT_ACCS = 8
NEG_BIG = -1e30

PB_Z, PB_XS, PB_Q, PB_GATT, PB_GSSD, PB_BM, PB_CM, PB_QIDX = (
    0, 2048, 4096, 5120, 6144, 7168, 7680, 8192)
PB_WIDTH = 8704
PF_KV, PF_KIDX, PF_DT = 0, 256, 384
PF_WIDTH = 512


def _params(*sem):
    return pltpu.CompilerParams(dimension_semantics=sem,
                                vmem_limit_bytes=VMEM_LIMIT_BYTES)


def _dot(a, b):
    return jnp.dot(a, b, preferred_element_type=F32)


def _dot_nt(a, b):
    return lax.dot_general(a, b, (((1,), (1,)), ((), ())),
                           preferred_element_type=F32)


def _rms_scale(x):
    return lax.rsqrt(jnp.mean(x * x, axis=-1, keepdims=True) + EPS)


def _mod_kernel(c_ref, w_ref, b_ref, o_ref):
    c = c_ref[...]
    c_act = (c * jax.nn.sigmoid(c)).astype(BF16)
    o_ref[...] = _dot(c_act, w_ref[...].astype(BF16)) + b_ref[...]


def _modulation(c, ada_w, ada_b):
    n_layers, d, n = ada_w.shape
    b = c.shape[0]
    rows = ((b + SUBLANES - 1) // SUBLANES) * SUBLANES
    c_pad = jnp.zeros((rows, d), F32).at[:b].set(c)
    out = pl.pallas_call(
        _mod_kernel,
        grid=(n_layers, n // MOD_TN),
        in_specs=[
            pl.BlockSpec((rows, d), lambda l, j: (0, 0)),
            pl.BlockSpec((None, d, MOD_TN), lambda l, j: (l, 0, j)),
            pl.BlockSpec((None, 1, MOD_TN), lambda l, j: (l, 0, j)),
        ],
        out_specs=pl.BlockSpec((None, rows, MOD_TN), lambda l, j: (l, 0, j)),
        out_shape=jax.ShapeDtypeStruct((n_layers, rows, n), F32),
        compiler_params=_params("arbitrary", "arbitrary"),
        name="adaln_mod",
    )(c_pad, ada_w, ada_b.reshape(n_layers, 1, n))
    return out[:, :b].reshape(n_layers, b, N_MOD, 1, d)


def _inproj_kernel(x_ref, g_ref, sc_ref, sh_ref, wb_ref, wf_ref,
                   ob_ref, of_ref, h_ref, *, n_bf16_tiles):
    j = pl.program_id(2)

    @pl.when(j == 0)
    def _():
        x = x_ref[...]
        h = x * _rms_scale(x) * g_ref[...]
        h = h * (1.0 + sc_ref[...]) + sh_ref[...]
        h_ref[...] = h.astype(BF16)

    @pl.when(j < n_bf16_tiles)
    def _():
        ob_ref[...] = _dot(h_ref[...], wb_ref[...]).astype(BF16)

    @pl.when(j == n_bf16_tiles)
    def _():
        of_ref[...] = _dot(h_ref[...], wf_ref[...])


def _input_projection(x, g, sc, sh, wb, wf):
    b, s, d = x.shape
    tm = min(INPROJ_TM, s)
    nj = PB_WIDTH // INPROJ_TN
    last = nj - 1
    return pl.pallas_call(
        functools.partial(_inproj_kernel, n_bf16_tiles=nj),
        grid=(b, s // tm, nj + 1),
        in_specs=[
            pl.BlockSpec((None, tm, d), lambda bi, i, j: (bi, i, 0)),
            pl.BlockSpec((1, d), lambda bi, i, j: (0, 0)),
            pl.BlockSpec((None, 1, d), lambda bi, i, j: (bi, 0, 0)),
            pl.BlockSpec((None, 1, d), lambda bi, i, j: (bi, 0, 0)),
            pl.BlockSpec((d, INPROJ_TN),
                         lambda bi, i, j: (0, jnp.minimum(j, last))),
            pl.BlockSpec((d, PF_WIDTH), lambda bi, i, j: (0, 0)),
        ],
        out_specs=[
            pl.BlockSpec((None, tm, INPROJ_TN),
                         lambda bi, i, j: (bi, i, jnp.minimum(j, last))),
            pl.BlockSpec((None, tm, PF_WIDTH), lambda bi, i, j: (bi, i, 0)),
        ],
        out_shape=[
            jax.ShapeDtypeStruct((b, s, PB_WIDTH), BF16),
            jax.ShapeDtypeStruct((b, s, PF_WIDTH), F32),
        ],
        scratch_shapes=[pltpu.VMEM((tm, d), BF16)],
        compiler_params=_params("arbitrary", "arbitrary", "arbitrary"),
        name="norm_inproj",
    )(x, g, sc, sh, wb, wf)


def _prep_kernel(kv_ref, kw_ref, gkv_ref, gk_ref, ckv_ref, ckvt_ref,
                 kidx_ref, wt_ref, *, n_chunks):
    kv = kv_ref[...]
    ckv = kv * _rms_scale(kv) * gkv_ref[...]
    ckv_ref[...] = ckv.astype(BF16)
    extra_row = lax.broadcasted_iota(I32, (CKVT_ROWS - KV_RANK, KEY_CHUNK), 0)
    extra = jnp.where(extra_row == 0, 1.0, 0.0).astype(BF16)
    for c in range(n_chunks):
        blk = ckv[c * KEY_CHUNK:(c + 1) * KEY_CHUNK, :]
        ckvt_ref[c] = jnp.concatenate([blk.T.astype(BF16), extra], axis=0)
    kw = kw_ref[...]
    k = kw[:, :IDX_DIM]
    kn = k * _rms_scale(k) * gk_ref[...]
    kidx_ref[...] = kn.astype(BF16)
    kwt = kw.T
    wt_ref[...] = kwt[IDX_DIM:IDX_DIM + IDX_HEADS, :] * (IDX_HEADS ** -0.5)


def _latent_prep(proj_f, kv_norm_g, kidx_norm_g):
    b, s, _ = proj_f.shape
    ts = min(1024, s)
    n_chunks = ts // KEY_CHUNK
    return pl.pallas_call(
        functools.partial(_prep_kernel, n_chunks=n_chunks),
        grid=(b, s // ts),
        in_specs=[
            pl.BlockSpec((None, ts, KV_RANK),
                         lambda bi, i: (bi, i, PF_KV // KV_RANK)),
            pl.BlockSpec((None, ts, LANES),
                         lambda bi, i: (bi, i, PF_KIDX // LANES)),
            pl.BlockSpec((1, KV_RANK), lambda bi, i: (0, 0)),
            pl.BlockSpec((1, IDX_DIM), lambda bi, i: (0, 0)),
        ],
        out_specs=[
            pl.BlockSpec((None, ts, KV_RANK), lambda bi, i: (bi, i, 0)),
            pl.BlockSpec((None, n_chunks, CKVT_ROWS, KEY_CHUNK),
                         lambda bi, i: (bi, i, 0, 0)),
            pl.BlockSpec((None, ts, IDX_DIM), lambda bi, i: (bi, i, 0)),
            pl.BlockSpec((None, IDX_HEADS, ts), lambda bi, i: (bi, 0, i)),
        ],
        out_shape=[
            jax.ShapeDtypeStruct((b, s, KV_RANK), BF16),
            jax.ShapeDtypeStruct((b, s // KEY_CHUNK, CKVT_ROWS, KEY_CHUNK), BF16),
            jax.ShapeDtypeStruct((b, s, IDX_DIM), BF16),
            jax.ShapeDtypeStruct((b, IDX_HEADS, s), F32),
        ],
        compiler_params=_params("arbitrary", "arbitrary"),
        name="latent_prep",
    )(proj_f, proj_f, kv_norm_g, kidx_norm_g)


def _dsa_kernel(q_ref, qidx_ref, wt_ref, ckv_ref, ckvt_ref, kidx_ref,
                wuk_ref, wuvt_ref, o_ref,
                keys_ref, khi_ref, klo_ref, lom_ref, qall_ref, qabs_ref, acc_ref,
                tsel_ref, p_ref,
                *, topk, pos_bits):
    qb = pl.program_id(1)
    n_chunks = (qb * Q_TILE + Q_TILE + KEY_CHUNK - 1) // KEY_CHUNK
    q_pos = qb * Q_TILE + lax.broadcasted_iota(I32, (KEY_CHUNK, Q_TILE), 1)
    row_iota = lax.broadcasted_iota(I32, (KEY_CHUNK, Q_TILE), 0)

    for h in range(IDX_HEADS):
        qall_ref[h * Q_TILE:(h + 1) * Q_TILE, :] = (
            qidx_ref[:, h * IDX_DIM:(h + 1) * IDX_DIM])

    n_pairs = (n_chunks + 1) // 2
    pair_rows = 2 * KEY_CHUNK

    def score_chunk(c):
        start = pl.multiple_of(c * KEY_CHUNK, KEY_CHUNK)
        ks = kidx_ref[pl.ds(start, KEY_CHUNK), :]
        logits = _dot_nt(ks, qall_ref[...])
        score = jnp.zeros((KEY_CHUNK, Q_TILE), F32)
        for h in range(IDX_HEADS):
            lg = logits[:, h * Q_TILE:(h + 1) * Q_TILE]
            score = score + wt_ref[h:h + 1, :] * jnp.maximum(lg, 0.0)
        bits = pltpu.bitcast(score, I32)
        key = bits ^ ((bits >> 31) & 0x7FFFFFFF)
        key = jnp.where(start + row_iota > q_pos, INT_MIN, key)
        keys_ref[pl.ds(start, KEY_CHUNK), :] = key
        khi_ref[pl.ds(start, KEY_CHUNK), :] = (key >> 16).astype(I16)
        klo_ref[pl.ds(start, KEY_CHUNK), :] = ((key & 0xFFFF) - HALF_BIAS).astype(I16)

    def score_pair(i, carry):
        score_chunk(2 * i)
        score_chunk(2 * i + 1)
        return carry

    lax.fori_loop(0, n_pairs, score_pair, 0)

    pair_iota = lax.broadcasted_iota(I32, (pair_rows, Q_TILE), 0)

    def count(pred):
        def body(c, accs):
            start = pl.multiple_of(c * pair_rows, pair_rows)
            kk = keys_ref[pl.ds(start, pair_rows), :]
            hit = jnp.where(pred(kk, start), 1, 0)
            accs = list(accs)
            for r in range(pair_rows // SUBLANES):
                k = r % COUNT_ACCS
                accs[k] = accs[k] + hit[r * SUBLANES:(r + 1) * SUBLANES]
            return tuple(accs)
        zeros = jnp.zeros((SUBLANES, Q_TILE), I32)
        accs = lax.fori_loop(0, n_pairs, body, (zeros,) * COUNT_ACCS)
        total = accs[0]
        for k in range(1, COUNT_ACCS):
            total = total + accs[k]
        return jnp.sum(total, axis=0, keepdims=True)

    zero = jnp.zeros((1, Q_TILE), I32)

    def count16(ref, t):
        t16 = jnp.broadcast_to(t, (PACK_ROWS, Q_TILE)).astype(I16)

        def body(c, accs):
            start = pl.multiple_of(c * pair_rows, pair_rows)
            blk = ref[pl.ds(start, pair_rows), :]
            accs = list(accs)
            for r in range(pair_rows // PACK_ROWS):
                hit = jnp.where(blk[r * PACK_ROWS:(r + 1) * PACK_ROWS] >= t16,
                                jnp.int16(1), jnp.int16(0))
                accs[r % COUNT_ACCS] = accs[r % COUNT_ACCS] + hit
            return tuple(accs)
        zeros16 = jnp.zeros((PACK_ROWS, Q_TILE), I16)
        accs = lax.fori_loop(0, n_pairs, body, (zeros16,) * COUNT_ACCS)
        total = accs[0].astype(I32)
        for k in range(1, COUNT_ACCS):
            total = total + accs[k].astype(I32)
        return jnp.sum(total, axis=0, keepdims=True)

    def select16(ref, target, floor_cnt):
        cnt0 = count16(ref, zero)
        ok = cnt0 >= target
        cand = jnp.where(ok, zero, jnp.full((1, Q_TILE), I16_MIN, I32))

        def bit_step(i, carry):
            cand, cand_cnt = carry
            trial = cand | (jnp.int32(1) << (14 - i))
            cnt = count16(ref, trial)
            keep = cnt >= target
            return jnp.where(keep, trial, cand), jnp.where(keep, cnt, cand_cnt)

        return lax.fori_loop(0, 15, bit_step, (cand, jnp.where(ok, cnt0, floor_cnt)))

    thr_hi, cnt_ge_hi = select16(khi_ref, topk, zero)
    cnt_above = count16(khi_ref, thr_hi + 1)
    real = thr_hi != I16_MIN

    thr_hi16 = jnp.broadcast_to(thr_hi, (PACK_ROWS, Q_TILE)).astype(I16)

    def mask_low(c, carry):
        start = pl.multiple_of(c * pair_rows, pair_rows)
        for r in range(pair_rows // PACK_ROWS):
            rows = pl.ds(start + r * PACK_ROWS, PACK_ROWS)
            lom_ref[rows, :] = jnp.where(khi_ref[rows, :] == thr_hi16,
                                         klo_ref[rows, :], jnp.int16(I16_MIN))
        return carry

    lax.fori_loop(0, n_pairs, mask_low, 0)
    thr_lo, cnt_lo = select16(lom_ref, topk - cnt_above, cnt_ge_hi - cnt_above)

    thr = jnp.where(real, (thr_hi << 16) | (thr_lo + HALF_BIAS), INT_MIN)
    thr_cnt = cnt_above + cnt_lo

    tsel_ref[...] = jnp.where(real, jnp.int32(2 ** 30), jnp.int32(-1))
    n_tied = jnp.max(jnp.where(real & (thr_cnt > topk), 1, 0))

    @pl.when(n_tied > 0)
    def _():
        need = topk - count(lambda kk, start: kk > thr)

        def pos_step(i, t):
            trial = t | (jnp.int32(1) << (pos_bits - 1 - i))
            cnt = count(lambda kk, start:
                        (kk == thr) & (start + pair_iota < trial))
            return jnp.where(cnt < need, trial, t)
        t = lax.fori_loop(0, pos_bits, pos_step, zero)
        tsel_ref[...] = jnp.where(real, t, jnp.int32(-1))

    tsel = tsel_ref[...]

    scale = ATT_HEAD_DIM ** -0.5 * LOG2_E
    for h in range(N_ATT_HEADS):
        qh = q_ref[:, h * ATT_HEAD_DIM:(h + 1) * ATT_HEAD_DIM]
        qa = _dot_nt(wuk_ref[h], qh) * scale
        qabs_ref[:, h * Q_TILE:(h + 1) * Q_TILE] = qa.astype(BF16)

    width = N_ATT_HEADS * Q_TILE
    acc_ref[...] = jnp.zeros_like(acc_ref)

    def probabilities(c, slot, m):
        start = pl.multiple_of(c * KEY_CHUNK, KEY_CHUNK)
        kv = ckv_ref[pl.ds(start, KEY_CHUNK), :]
        kk = keys_ref[pl.ds(start, KEY_CHUNK), :]
        pos = start + row_iota
        bias = jnp.where(
            kk > thr, 0.0,
            jnp.where(kk == thr, jnp.where(pos <= tsel, 0.0, NEG_BIG), NEG_BIG))
        m_out, alphas = [], []
        for grp in range(width // MXU_WIDTH):
            lo = grp * MXU_WIDTH
            st = _dot(kv, qabs_ref[:, lo:lo + MXU_WIDTH])
            for hh in range(MXU_WIDTH // Q_TILE):
                a, b = lo + hh * Q_TILE, lo + (hh + 1) * Q_TILE
                s_h = st[:, hh * Q_TILE:(hh + 1) * Q_TILE] + bias
                m_old = m[:, a:b]
                m_new = jnp.maximum(m_old, jnp.max(s_h, axis=0, keepdims=True))
                alpha = jnp.exp2(m_old - m_new)
                p = jnp.exp2(s_h - m_new)
                m_out.append(m_new)
                alphas.append(alpha)
                p_ref[slot, :, a:b] = p.astype(BF16)
        cat = lambda xs: jnp.concatenate(xs, axis=1)
        return cat(m_out), cat(alphas)

    def accumulate(c, slot, alpha):
        kvt = ckvt_ref[c]
        for grp in range(width // MXU_WIDTH):
            lo, hi = grp * MXU_WIDTH, (grp + 1) * MXU_WIDTH
            pv = _dot(kvt, p_ref[slot, :, lo:hi])
            acc_ref[:, lo:hi] = acc_ref[:, lo:hi] * alpha[:, lo:hi] + pv

    p_ref[1] = jnp.zeros((KEY_CHUNK, width), BF16)

    def attn_pair(i, carry):
        m, alpha_prev = carry
        c0 = 2 * i
        m, alpha0 = probabilities(c0, 0, m)
        accumulate(jnp.maximum(c0 - 1, 0), 1, alpha_prev)
        m, alpha1 = probabilities(c0 + 1, 1, m)
        accumulate(c0, 0, alpha0)
        return m, alpha1

    m0 = jnp.full((1, width), NEG_BIG, F32)
    ones = jnp.ones((1, width), F32)
    _, alpha_last = lax.fori_loop(0, n_pairs, attn_pair, (m0, ones))
    accumulate(2 * n_pairs - 1, 1, alpha_last)

    denom = acc_ref[KV_RANK:KV_RANK + 1, :]
    o_lat = (acc_ref[0:KV_RANK, :] * (1.0 / denom)).astype(BF16)
    for h in range(N_ATT_HEADS):
        ot = _dot(wuvt_ref[h], o_lat[:, h * Q_TILE:(h + 1) * Q_TILE])
        o_ref[:, h * ATT_HEAD_DIM:(h + 1) * ATT_HEAD_DIM] = ot.T.astype(BF16)


def _dsa_attention(proj_b, wt, ckv, ckvt, kidx, wuk, wuvt):
    b, s, _ = proj_b.shape
    topk = min(TOPK_MAX, s // 4)
    assert s % (2 * KEY_CHUNK) == 0
    width = N_ATT_HEADS * Q_TILE
    return pl.pallas_call(
        functools.partial(_dsa_kernel, topk=topk, pos_bits=(s - 1).bit_length()),
        grid=(b, s // Q_TILE),
        in_specs=[
            pl.BlockSpec((None, Q_TILE, ATT_WIDTH),
                         lambda bi, i: (bi, i, PB_Q // ATT_WIDTH)),
            pl.BlockSpec((None, Q_TILE, IDX_HEADS * IDX_DIM),
                         lambda bi, i: (bi, i, PB_QIDX // (IDX_HEADS * IDX_DIM))),
            pl.BlockSpec((None, IDX_HEADS, Q_TILE), lambda bi, i: (bi, 0, i)),
            pl.BlockSpec((None, s, KV_RANK), lambda bi, i: (bi, 0, 0)),
            pl.BlockSpec((None, s // KEY_CHUNK, CKVT_ROWS, KEY_CHUNK),
                         lambda bi, i: (bi, 0, 0, 0)),
            pl.BlockSpec((None, s, IDX_DIM), lambda bi, i: (bi, 0, 0)),
            pl.BlockSpec((N_ATT_HEADS, KV_RANK, ATT_HEAD_DIM),
                         lambda bi, i: (0, 0, 0)),
            pl.BlockSpec((N_ATT_HEADS, ATT_HEAD_DIM, KV_RANK),
                         lambda bi, i: (0, 0, 0)),
        ],
        out_specs=pl.BlockSpec((None, Q_TILE, ATT_WIDTH), lambda bi, i: (bi, i, 0)),
        out_shape=jax.ShapeDtypeStruct((b, s, ATT_WIDTH), BF16),
        scratch_shapes=[
            pltpu.VMEM((s, Q_TILE), I32),
            pltpu.VMEM((s, Q_TILE), I16),
            pltpu.VMEM((s, Q_TILE), I16),
            pltpu.VMEM((s, Q_TILE), I16),
            pltpu.VMEM((IDX_HEADS * Q_TILE, IDX_DIM), BF16),
            pltpu.VMEM((KV_RANK, width), BF16),
            pltpu.VMEM((CKVT_ROWS, width), F32),
            pltpu.VMEM((1, Q_TILE), I32),
            pltpu.VMEM((2, KEY_CHUNK, width), BF16),
        ],
        compiler_params=_params("arbitrary", "arbitrary"),
        name="dsa_attention",
    )(proj_b, proj_b, wt, ckv, ckvt, kidx, wuk, wuvt)


def _ssd_kernel(z_ref, xs_ref, bm_ref, cm_ref, dt_ref, shift_ref,
                cwx_ref, cwb_ref, cwc_ref, cbx_ref, cbb_ref, cbc_ref,
                dtb_ref, alog_ref, dskip_ref, ng_ref, y_ref,
                tx_ref, tb_ref, tc_ref, state_ref, yacc_ref, *, chunk):
    ci = pl.program_id(1)
    taps = CONV_WIDTH - 1

    @pl.when(ci == 0)
    def _():
        tx_ref[...] = jnp.zeros_like(tx_ref)
        tb_ref[...] = jnp.zeros_like(tb_ref)
        tc_ref[...] = jnp.zeros_like(tc_ref)
        state_ref[...] = jnp.zeros_like(state_ref)

    def conv_silu(in_ref, tail_ref, w_ref, b_ref):
        cur = in_ref[...]
        shifted = _dot(shift_ref[...], cur)
        cur_f = cur.astype(F32)
        acc = b_ref[...] + w_ref[taps:taps + 1, :] * cur_f
        for k in range(taps):
            acc = acc + w_ref[k:k + 1, :] * shifted[k * chunk:(k + 1) * chunk]
        tail = tail_ref[...]
        row = lax.broadcasted_iota(I32, tail.shape, 0)
        fix = jnp.zeros_like(tail)
        for k in range(taps):
            back = taps - k
            fix = fix + jnp.where(row < back,
                                  w_ref[k:k + 1, :] * pltpu.roll(tail, back, 0), 0.0)
        acc = jnp.concatenate([acc[:SUBLANES] + fix, acc[SUBLANES:]], axis=0)
        tail_ref[...] = cur_f[chunk - SUBLANES:]
        return acc * jax.nn.sigmoid(acc)

    xs = conv_silu(xs_ref, tx_ref, cwx_ref, cbx_ref)
    bm = conv_silu(bm_ref, tb_ref, cwb_ref, cbb_ref)
    cm = conv_silu(cm_ref, tc_ref, cwc_ref, cbc_ref)
    xs_b = xs.astype(BF16)

    dt = jax.nn.softplus(dt_ref[...] + dtb_ref[...])
    a2 = -jnp.exp(alog_ref[...]) * LOG2_E
    r_io = lax.broadcasted_iota(I32, (chunk, chunk), 0)
    c_io = lax.broadcasted_iota(I32, (chunk, chunk), 1)
    tril = r_io >= c_io
    a_cum = jnp.dot(tril.astype(F32), dt * a2, precision=lax.Precision.HIGHEST,
                    preferred_element_type=F32)
    total = a_cum[chunk - 1:chunk, :]
    to_end_t = (jnp.exp2(total - a_cum) * dt).T
    src_t = (a_cum - jnp.log2(dt)).T
    first = lax.broadcasted_iota(I32, (1, LANES), 1) < SSM_HEAD_DIM

    for g in range(N_SSM_GROUPS):
        bg = bm[:, g * D_STATE:(g + 1) * D_STATE]
        cg = cm[:, g * D_STATE:(g + 1) * D_STATE]
        cb = _dot_nt(cg.astype(BF16), bg.astype(BF16))
        bg_t = bg.T
        for pair in range(HEADS_PER_GROUP // 2):
            h0 = g * HEADS_PER_GROUP + 2 * pair
            lo = h0 * SSM_HEAD_DIM
            lhs, bgw, e_last = [], [], []
            for h in (h0, h0 + 1):
                col = jnp.broadcast_to(a_cum[:, h:h + 1], (chunk, chunk))
                decay = jnp.exp2(jnp.where(tril, col - src_t[h:h + 1, :], -jnp.inf))
                ecol = jnp.exp2(col)
                lhs.append(jnp.concatenate(
                    [(cb * decay).astype(BF16), (cg * ecol).astype(BF16)], axis=1))
                bgw.append((bg_t * to_end_t[h:h + 1, :]).astype(BF16))
                e_last.append(ecol[chunk - 1:chunk, :])
            xs_pair = xs_b[:, lo:lo + LANES]
            prev = state_ref[:, lo:lo + LANES]
            rhs = jnp.concatenate([xs_pair, prev.astype(BF16)], axis=0)
            r = _dot(jnp.concatenate(lhs, axis=0), rhs)
            yacc_ref[:, lo:lo + LANES] = jnp.where(first, r[:chunk], r[chunk:])
            nw = _dot(jnp.concatenate(bgw, axis=0), xs_pair)
            keep = jnp.where(first, e_last[0], e_last[1])
            state_ref[:, lo:lo + LANES] = prev * keep + jnp.where(
                first, nw[:D_STATE], nw[D_STATE:])

    y = yacc_ref[...] + xs * dskip_ref[...]
    zf = z_ref[...].astype(F32)
    y = y * (zf * jax.nn.sigmoid(zf))
    gw = D_INNER // N_SSM_GROUPS
    for g in range(N_SSM_GROUPS):
        yg = y[:, g * gw:(g + 1) * gw]
        yn = yg * _rms_scale(yg) * ng_ref[:, g * gw:(g + 1) * gw]
        y_ref[:, g * gw:(g + 1) * gw] = yn.astype(BF16)


def _ssd(proj_b, proj_f, conv_w, conv_b, dt_bias, a_log, d_skip, ssm_norm_g):
    b, s, _ = proj_b.shape
    chunk = SSD_CHUNK
    assert chunk == D_STATE == LANES and s % chunk == 0
    pad = LANES - N_SSM_HEADS
    taps = CONV_WIDTH - 1
    t_io = np.arange(chunk)
    shift = np.concatenate(
        [(t_io[:, None] - (taps - k) == t_io[None, :]) for k in range(taps)], axis=0)
    shift = jnp.asarray(shift, BF16)
    cw_x, cw_b, cw_c = (conv_w[:, :D_INNER], conv_w[:, D_INNER:D_INNER + BC_WIDTH],
                        conv_w[:, D_INNER + BC_WIDTH:])
    cb = conv_b.reshape(1, -1)
    cb_x, cb_b, cb_c = (cb[:, :D_INNER], cb[:, D_INNER:D_INNER + BC_WIDTH],
                        cb[:, D_INNER + BC_WIDTH:])
    dtb = jnp.pad(dt_bias, (0, pad)).reshape(1, LANES)
    alog = jnp.pad(a_log, (0, pad)).reshape(1, LANES)
    dskip = jnp.repeat(d_skip, SSM_HEAD_DIM).reshape(1, D_INNER)
    const = lambda bi, i: (0, 0)
    return pl.pallas_call(
        functools.partial(_ssd_kernel, chunk=chunk),
        grid=(b, s // chunk),
        in_specs=[
            pl.BlockSpec((None, chunk, D_INNER), lambda bi, i: (bi, i, PB_Z // D_INNER)),
            pl.BlockSpec((None, chunk, D_INNER), lambda bi, i: (bi, i, PB_XS // D_INNER)),
            pl.BlockSpec((None, chunk, BC_WIDTH), lambda bi, i: (bi, i, PB_BM // BC_WIDTH)),
            pl.BlockSpec((None, chunk, BC_WIDTH), lambda bi, i: (bi, i, PB_CM // BC_WIDTH)),
            pl.BlockSpec((None, chunk, LANES), lambda bi, i: (bi, i, PF_DT // LANES)),
            pl.BlockSpec((taps * chunk, chunk), const),
            pl.BlockSpec((CONV_WIDTH, D_INNER), const),
            pl.BlockSpec((CONV_WIDTH, BC_WIDTH), const),
            pl.BlockSpec((CONV_WIDTH, BC_WIDTH), const),
            pl.BlockSpec((1, D_INNER), const),
            pl.BlockSpec((1, BC_WIDTH), const),
            pl.BlockSpec((1, BC_WIDTH), const),
            pl.BlockSpec((1, LANES), const),
            pl.BlockSpec((1, LANES), const),
            pl.BlockSpec((1, D_INNER), const),
            pl.BlockSpec((1, D_INNER), const),
        ],
        out_specs=pl.BlockSpec((None, chunk, D_INNER), lambda bi, i: (bi, i, 0)),
        out_shape=jax.ShapeDtypeStruct((b, s, D_INNER), BF16),
        scratch_shapes=[
            pltpu.VMEM((SUBLANES, D_INNER), F32),
            pltpu.VMEM((SUBLANES, BC_WIDTH), F32),
            pltpu.VMEM((SUBLANES, BC_WIDTH), F32),
            pltpu.VMEM((D_STATE, D_INNER), F32),
            pltpu.VMEM((chunk, D_INNER), F32),
        ],
        compiler_params=_params("arbitrary", "arbitrary"),
        name="ssd_mixer",
    )(proj_b, proj_b, proj_b, proj_b, proj_f, shift, cw_x, cw_b, cw_c, cb_x, cb_b, cb_c,
      dtb, alog, dskip, ssm_norm_g.reshape(1, D_INNER))


def _mix_out_kernel(att_ref, y_ref, ga_ref, gs_ref, x_ref, gate_ref, g_ref,
                    woa_ref, wos_ref, wout_ref, o_ref):
    ba = _dot(att_ref[...], woa_ref[...])
    bs = _dot(y_ref[...], wos_ref[...])
    merged = (jax.nn.sigmoid(ga_ref[...].astype(F32)) * ba
              + jax.nn.sigmoid(gs_ref[...].astype(F32)) * bs)
    out = _dot(merged.astype(BF16), wout_ref[...])
    normed = out * _rms_scale(out) * g_ref[...]
    o_ref[...] = x_ref[...] + gate_ref[...] * normed


def _mix_out(att, y, proj_b, x, gate, g, woa, wos, wout):
    b, s, d = x.shape
    tm = min(OUT_TM, s)
    const = lambda bi, i: (0, 0)
    return pl.pallas_call(
        _mix_out_kernel,
        grid=(b, s // tm),
        in_specs=[
            pl.BlockSpec((None, tm, ATT_WIDTH), lambda bi, i: (bi, i, 0)),
            pl.BlockSpec((None, tm, D_INNER), lambda bi, i: (bi, i, 0)),
            pl.BlockSpec((None, tm, d), lambda bi, i: (bi, i, PB_GATT // d)),
            pl.BlockSpec((None, tm, d), lambda bi, i: (bi, i, PB_GSSD // d)),
            pl.BlockSpec((None, tm, d), lambda bi, i: (bi, i, 0)),
            pl.BlockSpec((None, 1, d), lambda bi, i: (bi, 0, 0)),
            pl.BlockSpec((1, d), const),
            pl.BlockSpec((ATT_WIDTH, d), const),
            pl.BlockSpec((D_INNER, d), const),
            pl.BlockSpec((d, d), const),
        ],
        out_specs=pl.BlockSpec((None, tm, d), lambda bi, i: (bi, i, 0)),
        out_shape=jax.ShapeDtypeStruct((b, s, d), F32),
        compiler_params=_params("arbitrary", "arbitrary"),
        name="mix_out",
    )(att, y, proj_b, proj_b, x, gate, g, woa, wos, wout)


def _mlp_kernel(x_ref, g_in_ref, sc_ref, sh_ref, wup_ref, wdn_ref, gate_ref,
                g_out_ref, o_ref, h_ref, acc_ref):
    j = pl.program_id(2)

    @pl.when(j == 0)
    def _():
        x = x_ref[...]
        h = x * _rms_scale(x) * g_in_ref[...]
        h = h * (1.0 + sc_ref[...]) + sh_ref[...]
        h_ref[...] = h.astype(BF16)
        acc_ref[...] = jnp.zeros_like(acc_ref)

    up = jnp.maximum(_dot(h_ref[...], wup_ref[...]), 0.0)
    acc_ref[...] += _dot((up * up).astype(BF16), wdn_ref[...])

    @pl.when(j == pl.num_programs(2) - 1)
    def _():
        y = acc_ref[...]
        normed = y * _rms_scale(y) * g_out_ref[...]
        o_ref[...] = x_ref[...] + gate_ref[...] * normed


def _mlp(x, g_in, sc, sh, wup, wdn, gate, g_out):
    b, s, d = x.shape
    ff = wup.shape[1]
    tm = min(MLP_TM, s)
    per_b = lambda bi, i, j: (bi, 0, 0)
    const = lambda bi, i, j: (0, 0)
    return pl.pallas_call(
        _mlp_kernel,
        grid=(b, s // tm, ff // MLP_TF),
        in_specs=[
            pl.BlockSpec((None, tm, d), lambda bi, i, j: (bi, i, 0)),
            pl.BlockSpec((1, d), const),
            pl.BlockSpec((None, 1, d), per_b),
            pl.BlockSpec((None, 1, d), per_b),
            pl.BlockSpec((d, MLP_TF), lambda bi, i, j: (0, j)),
            pl.BlockSpec((MLP_TF, d), lambda bi, i, j: (j, 0)),
            pl.BlockSpec((None, 1, d), per_b),
            pl.BlockSpec((1, d), const),
        ],
        out_specs=pl.BlockSpec((None, tm, d), lambda bi, i, j: (bi, i, 0)),
        out_shape=jax.ShapeDtypeStruct((b, s, d), F32),
        scratch_shapes=[pltpu.VMEM((tm, d), BF16), pltpu.VMEM((tm, d), F32)],
        compiler_params=_params("arbitrary", "arbitrary", "arbitrary"),
        name="mlp",
    )(x, g_in, sc, sh, wup, wdn, gate, g_out)


def _split_offsets():
    sizes = (ATT_WIDTH, KV_RANK, IDX_HEADS * IDX_DIM, IDX_DIM, IDX_HEADS,
             D_INNER, D_INNER, BC_WIDTH, BC_WIDTH, N_SSM_HEADS, ATT_WIDTH, ATT_WIDTH)
    names = ("q", "kv", "qidx", "kidx", "widx", "z", "xs", "bm", "cm", "dt",
             "gatt", "gssd")
    offs = np.concatenate([[0], np.cumsum(sizes)])
    return {n: (int(offs[i]), int(offs[i + 1])) for i, n in enumerate(names)}


def _pack_w_in(w_in):
    sl = _split_offsets()
    col = lambda n: w_in[..., sl[n][0]:sl[n][1]]
    wb = jnp.concatenate([col(n) for n in
                          ("z", "xs", "q", "gatt", "gssd", "bm", "cm", "qidx")],
                         axis=-1).astype(BF16)
    lead = w_in.shape[:-1]
    zeros = lambda n: jnp.zeros(lead + (n,), w_in.dtype)
    wf = jnp.concatenate([
        col("kv"), col("kidx"), col("widx"), zeros(LANES - IDX_DIM - IDX_HEADS),
        col("dt"), zeros(LANES - N_SSM_HEADS)], axis=-1).astype(BF16)
    return wb, wf


def kernel(x, c, ada_w, ada_b, norm_g, w_in, kv_norm_g, kidx_norm_g, w_uk, w_uv,
           conv_w, conv_b, dt_bias, a_log, d_skip, ssm_norm_g, w_o_att, w_o_ssd,
           w_out, w_up, w_down):
    depth = ada_w.shape[0]
    d = x.shape[-1]
    mod = _modulation(c, ada_w, ada_b)
    wb_all, wf_all = _pack_w_in(w_in)
    wuk_all = jnp.transpose(w_uk, (0, 2, 1, 3)).astype(BF16)
    wuvt_all = jnp.transpose(w_uv, (0, 2, 3, 1)).astype(BF16)
    woa_all = w_o_att.astype(BF16)
    wos_all = w_o_ssd.astype(BF16)
    wout_all = w_out.astype(BF16)
    wup_all = w_up.astype(BF16)
    wdn_all = w_down.astype(BF16)

    for layer in range(depth):
        sh_m, sc_m, gt_m, sh_f, sc_f, gt_f = (mod[layer, :, k] for k in range(N_MOD))
        ng = lambda k: norm_g[layer, k].reshape(1, d)
        proj_b, proj_f = _input_projection(x, ng(0), sc_m, sh_m,
                                           wb_all[layer], wf_all[layer])
        ckv, ckvt, kidx, wt = _latent_prep(
            proj_f, kv_norm_g[layer].reshape(1, KV_RANK),
            kidx_norm_g[layer].reshape(1, IDX_DIM))
        att = _dsa_attention(proj_b, wt, ckv, ckvt, kidx,
                             wuk_all[layer], wuvt_all[layer])
        y = _ssd(proj_b, proj_f, conv_w[layer], conv_b[layer], dt_bias[layer],
                 a_log[layer], d_skip[layer], ssm_norm_g[layer])
        x = _mix_out(att, y, proj_b, x, gt_m, ng(1), woa_all[layer],
                     wos_all[layer], wout_all[layer])
        x = _mlp(x, ng(2), sc_f, sh_f, wup_all[layer], wdn_all[layer], gt_f, ng(3))
    return x
```

```python
import functools

import jax
import jax.numpy as jnp
import numpy as np
from jax import lax
from jax.experimental import pallas as pl
from jax.experimental.pallas import tpu as pltpu

F32 = jnp.float32
BF16 = jnp.bfloat16
I32 = jnp.int32
I16 = jnp.int16

N_ATT_HEADS = 8
ATT_HEAD_DIM = 128
ATT_WIDTH = N_ATT_HEADS * ATT_HEAD_DIM
KV_RANK = 256
IDX_HEADS = 8
IDX_DIM = 64
TOPK_MAX = 256
D_STATE = 128
SSM_HEAD_DIM = 64
N_SSM_GROUPS = 4
HEADS_PER_GROUP = 8
N_SSM_HEADS = N_SSM_GROUPS * HEADS_PER_GROUP
D_INNER = N_SSM_HEADS * SSM_HEAD_DIM
BC_WIDTH = N_SSM_GROUPS * D_STATE
CONV_WIDTH = 4
N_MOD = 6
EPS = 1e-6

LANES = 128
SUBLANES = 8
VMEM_LIMIT_BYTES = 52 * 1024 * 1024

MXU_WIDTH = 256
Q_TILE = 256
KEY_CHUNK = 256
CKVT_ROWS = KV_RANK + 16
SSD_CHUNK = 128
INPROJ_TM = 2048
INPROJ_TN = 512
OUT_TM = 512
MLP_TM = 1024
MLP_TF = 1024
MOD_TN = 1536

INT_MIN = -2147483648
I16_MIN = -32768
I16_MAX = 32767
HALF_BIAS = 32768
PACK_ROWS = 16
LOG2_E = 1.4426950408889634
COUNT_ACCS = 8
NEG_BIG = -1e30

PB_Z, PB_XS, PB_Q, PB_GATT, PB_GSSD, PB_BM, PB_CM, PB_QIDX = (
    0, 2048, 4096, 5120, 6144, 7168, 7680, 8192)
PB_WIDTH = 8704
PF_KV, PF_KIDX, PF_DT = 0, 256, 384
PF_WIDTH = 512


def _params(*sem):
    return pltpu.CompilerParams(dimension_semantics=sem,
                                vmem_limit_bytes=VMEM_LIMIT_BYTES)


def _dot(a, b):
    return jnp.dot(a, b, preferred_element_type=F32)


def _dot_nt(a, b):
    return lax.dot_general(a, b, (((1,), (1,)), ((), ())),
                           preferred_element_type=F32)


def _silu(x):
    return x * jax.nn.sigmoid(x)


def _rms_scale(x):
    return lax.rsqrt(jnp.mean(x * x, axis=-1, keepdims=True) + EPS)


def _mod_kernel(c_ref, w_ref, b_ref, o_ref):
    c = c_ref[...]
    c_act = (c * jax.nn.sigmoid(c)).astype(BF16)
    o_ref[...] = _dot(c_act, w_ref[...].astype(BF16)) + b_ref[...]


def _modulation(c, ada_w, ada_b):
    n_layers, d, n = ada_w.shape
    b = c.shape[0]
    rows = ((b + SUBLANES - 1) // SUBLANES) * SUBLANES
    c_pad = jnp.zeros((rows, d), F32).at[:b].set(c)
    out = pl.pallas_call(
        _mod_kernel,
        grid=(n_layers, n // MOD_TN),
        in_specs=[
            pl.BlockSpec((rows, d), lambda l, j: (0, 0)),
            pl.BlockSpec((None, d, MOD_TN), lambda l, j: (l, 0, j)),
            pl.BlockSpec((None, 1, MOD_TN), lambda l, j: (l, 0, j)),
        ],
        out_specs=pl.BlockSpec((None, rows, MOD_TN), lambda l, j: (l, 0, j)),
        out_shape=jax.ShapeDtypeStruct((n_layers, rows, n), F32),
        compiler_params=_params("arbitrary", "arbitrary"),
        name="adaln_mod",
    )(c_pad, ada_w, ada_b.reshape(n_layers, 1, n))
    return out[:, :b].reshape(n_layers, b, N_MOD, 1, d)


def _inproj_kernel(x_ref, g_ref, sc_ref, sh_ref, wb_ref, wf_ref,
                   ob_ref, of_ref, h_ref, *, n_bf16_tiles):
    j = pl.program_id(2)

    @pl.when(j == 0)
    def _():
        x = x_ref[...]
        h = x * _rms_scale(x) * g_ref[...]
        h = h * (1.0 + sc_ref[...]) + sh_ref[...]
        h_ref[...] = h.astype(BF16)

    @pl.when(j < n_bf16_tiles)
    def _():
        ob_ref[...] = _dot(h_ref[...], wb_ref[...]).astype(BF16)

    @pl.when(j == n_bf16_tiles)
    def _():
        of_ref[...] = _dot(h_ref[...], wf_ref[...])


def _input_projection(x, g, sc, sh, wb, wf):
    b, s, d = x.shape
    tm = min(INPROJ_TM, s)
    nj = PB_WIDTH // INPROJ_TN
    last = nj - 1
    return pl.pallas_call(
        functools.partial(_inproj_kernel, n_bf16_tiles=nj),
        grid=(b, s // tm, nj + 1),
        in_specs=[
            pl.BlockSpec((None, tm, d), lambda bi, i, j: (bi, i, 0)),
            pl.BlockSpec((1, d), lambda bi, i, j: (0, 0)),
            pl.BlockSpec((None, 1, d), lambda bi, i, j: (bi, 0, 0)),
            pl.BlockSpec((None, 1, d), lambda bi, i, j: (bi, 0, 0)),
            pl.BlockSpec((d, INPROJ_TN),
                         lambda bi, i, j: (0, jnp.minimum(j, last))),
            pl.BlockSpec((d, PF_WIDTH), lambda bi, i, j: (0, 0)),
        ],
        out_specs=[
            pl.BlockSpec((None, tm, INPROJ_TN),
                         lambda bi, i, j: (bi, i, jnp.minimum(j, last))),
            pl.BlockSpec((None, tm, PF_WIDTH), lambda bi, i, j: (bi, i, 0)),
        ],
        out_shape=[
            jax.ShapeDtypeStruct((b, s, PB_WIDTH), BF16),
            jax.ShapeDtypeStruct((b, s, PF_WIDTH), F32),
        ],
        scratch_shapes=[pltpu.VMEM((tm, d), BF16)],
        compiler_params=_params("arbitrary", "arbitrary", "arbitrary"),
        name="norm_inproj",
    )(x, g, sc, sh, wb, wf)


def _prep_kernel(kv_ref, kw_ref, gkv_ref, gk_ref, ckv_ref, ckvt_ref,
                 kidx_ref, wt_ref, *, n_chunks):
    kv = kv_ref[...]
    ckv = kv * _rms_scale(kv) * gkv_ref[...]
    ckv_ref[...] = ckv.astype(BF16)
    extra_row = lax.broadcasted_iota(I32, (CKVT_ROWS - KV_RANK, KEY_CHUNK), 0)
    extra = jnp.where(extra_row == 0, 1.0, 0.0).astype(BF16)
    for c in range(n_chunks):
        blk = ckv[c * KEY_CHUNK:(c + 1) * KEY_CHUNK, :]
        ckvt_ref[c] = jnp.concatenate([blk.T.astype(BF16), extra], axis=0)
    kw = kw_ref[...]
    k = kw[:, :IDX_DIM]
    kn = k * _rms_scale(k) * gk_ref[...]
    kidx_ref[...] = kn.astype(BF16)
    kwt = kw.T
    wt_ref[...] = kwt[IDX_DIM:IDX_DIM + IDX_HEADS, :] * (IDX_HEADS ** -0.5)


def _latent_prep(proj_f, kv_norm_g, kidx_norm_g):
    b, s, _ = proj_f.shape
    ts = min(1024, s)
    n_chunks = ts // KEY_CHUNK
    return pl.pallas_call(
        functools.partial(_prep_kernel, n_chunks=n_chunks),
        grid=(b, s // ts),
        in_specs=[
            pl.BlockSpec((None, ts, KV_RANK),
                         lambda bi, i: (bi, i, PF_KV // KV_RANK)),
            pl.BlockSpec((None, ts, LANES),
                         lambda bi, i: (bi, i, PF_KIDX // LANES)),
            pl.BlockSpec((1, KV_RANK), lambda bi, i: (0, 0)),
            pl.BlockSpec((1, IDX_DIM), lambda bi, i: (0, 0)),
        ],
        out_specs=[
            pl.BlockSpec((None, ts, KV_RANK), lambda bi, i: (bi, i, 0)),
            pl.BlockSpec((None, n_chunks, CKVT_ROWS, KEY_CHUNK),
                         lambda bi, i: (bi, i, 0, 0)),
            pl.BlockSpec((None, ts, IDX_DIM), lambda bi, i: (bi, i, 0)),
            pl.BlockSpec((None, IDX_HEADS, ts), lambda bi, i: (bi, 0, i)),
        ],
        out_shape=[
            jax.ShapeDtypeStruct((b, s, KV_RANK), BF16),
            jax.ShapeDtypeStruct((b, s // KEY_CHUNK, CKVT_ROWS, KEY_CHUNK), BF16),
            jax.ShapeDtypeStruct((b, s, IDX_DIM), BF16),
            jax.ShapeDtypeStruct((b, IDX_HEADS, s), F32),
        ],
        compiler_params=_params("arbitrary", "arbitrary"),
        name="latent_prep",
    )(proj_f, proj_f, kv_norm_g, kidx_norm_g)


def _dsa_kernel(q_ref, qidx_ref, wt_ref, ckv_ref, ckvt_ref, kidx_ref,
                wuk_ref, wuvt_ref, o_ref,
                keys_ref, khi_ref, klo_ref, lom_ref, qall_ref, qabs_ref, acc_ref,
                tsel_ref, p_ref,
                *, topk, pos_bits):
    qb = pl.program_id(1)
    n_chunks = (qb * Q_TILE + Q_TILE + KEY_CHUNK - 1) // KEY_CHUNK
    q_pos = qb * Q_TILE + lax.broadcasted_iota(I32, (KEY_CHUNK, Q_TILE), 1)
    row_iota = lax.broadcasted_iota(I32, (KEY_CHUNK, Q_TILE), 0)

    for h in range(IDX_HEADS):
        qall_ref[h * Q_TILE:(h + 1) * Q_TILE, :] = (
            qidx_ref[:, h * IDX_DIM:(h + 1) * IDX_DIM])

    n_pairs = (n_chunks + 1) // 2
    pair_rows = 2 * KEY_CHUNK

    def score_chunk(c):
        start = pl.multiple_of(c * KEY_CHUNK, KEY_CHUNK)
        ks = kidx_ref[pl.ds(start, KEY_CHUNK), :]
        logits = _dot_nt(ks, qall_ref[...])
        score = jnp.zeros((KEY_CHUNK, Q_TILE), F32)
        for h in range(IDX_HEADS):
            lg = logits[:, h * Q_TILE:(h + 1) * Q_TILE]
            score = score + wt_ref[h:h + 1, :] * jnp.maximum(lg, 0.0)
        bits = pltpu.bitcast(score, I32)
        key = bits ^ ((bits >> 31) & 0x7FFFFFFF)
        key = jnp.where(start + row_iota > q_pos, INT_MIN, key)
        keys_ref[pl.ds(start, KEY_CHUNK), :] = key
        khi_ref[pl.ds(start, KEY_CHUNK), :] = (key >> 16).astype(I16)
        klo_ref[pl.ds(start, KEY_CHUNK), :] = ((key & 0xFFFF) - HALF_BIAS).astype(I16)

    def score_pair(i, carry):
        score_chunk(2 * i)
        score_chunk(2 * i + 1)
        return carry

    lax.fori_loop(0, n_pairs, score_pair, 0)

    zero = jnp.zeros((1, Q_TILE), I32)

    def count16(ref, t):
        t16 = jnp.broadcast_to(t, (PACK_ROWS, Q_TILE)).astype(I16)

        def body(c, accs):
            start = pl.multiple_of(c * pair_rows, pair_rows)
            blk = ref[pl.ds(start, pair_rows), :]
            accs = list(accs)
            for r in range(pair_rows // PACK_ROWS):
                hit = jnp.where(blk[r * PACK_ROWS:(r + 1) * PACK_ROWS] >= t16,
                                jnp.int16(1), jnp.int16(0))
                accs[r % COUNT_ACCS] = accs[r % COUNT_ACCS] + hit
            return tuple(accs)
        zeros16 = jnp.zeros((PACK_ROWS, Q_TILE), I16)
        accs = lax.fori_loop(0, n_pairs, body, (zeros16,) * COUNT_ACCS)
        total = accs[0].astype(I32)
        for k in range(1, COUNT_ACCS):
            total = total + accs[k].astype(I32)
        return jnp.sum(total, axis=0, keepdims=True)

    def refine16(ref, target, cand, cand_cnt, n_bits):
        def bit_step(i, carry):
            cand, cand_cnt = carry
            trial = cand | (jnp.int32(1) << (n_bits - 1 - i))
            cnt = count16(ref, trial)
            keep = cnt >= target
            return jnp.where(keep, trial, cand), jnp.where(keep, cnt, cand_cnt)

        return lax.fori_loop(0, n_bits, bit_step, (cand, cand_cnt))

    def select16(ref, target, floor_cnt):
        cnt0 = count16(ref, zero)
        ok = cnt0 >= target
        cand = jnp.where(ok, zero, jnp.full((1, Q_TILE), I16_MIN, I32))
        return refine16(ref, target, cand, jnp.where(ok, cnt0, floor_cnt), 15)

    thr_hi, cnt_ge_hi = select16(khi_ref, topk, zero)
    cnt_above = count16(khi_ref, thr_hi + 1)
    real = thr_hi != I16_MIN

    thr_hi16 = jnp.broadcast_to(thr_hi, (PACK_ROWS, Q_TILE)).astype(I16)

    def mask_low(c, carry):
        start = pl.multiple_of(c * pair_rows, pair_rows)
        for r in range(pair_rows // PACK_ROWS):
            rows = pl.ds(start + r * PACK_ROWS, PACK_ROWS)
            lom_ref[rows, :] = jnp.where(khi_ref[rows, :] == thr_hi16,
                                         klo_ref[rows, :], jnp.int16(I16_MIN))
        return carry

    lax.fori_loop(0, n_pairs, mask_low, 0)
    thr_lo, cnt_lo = select16(lom_ref, topk - cnt_above, cnt_ge_hi - cnt_above)

    thr = jnp.where(real, (thr_hi << 16) | (thr_lo + HALF_BIAS), INT_MIN)
    thr_cnt = cnt_above + cnt_lo

    tsel_ref[...] = jnp.where(real, jnp.int32(2 ** 30), jnp.int32(-1))
    n_tied = jnp.max(jnp.where(real & (thr_cnt > topk), 1, 0))

    @pl.when(n_tied > 0)
    def _():
        cnt_gt = cnt_above + jnp.where(
            thr_lo < I16_MAX, count16(lom_ref, thr_lo + 1), 0)
        need = topk - cnt_gt
        thr_lo16 = jnp.broadcast_to(thr_lo, (PACK_ROWS, Q_TILE)).astype(I16)
        pack_iota = lax.broadcasted_iota(I32, (PACK_ROWS, Q_TILE), 0)

        def reversed_positions(c, carry):
            start = pl.multiple_of(c * pair_rows, pair_rows)
            for r in range(pair_rows // PACK_ROWS):
                rows = pl.ds(start + r * PACK_ROWS, PACK_ROWS)
                tied = ((khi_ref[rows, :] == thr_hi16)
                        & (lom_ref[rows, :] == thr_lo16))
                rev = (I16_MAX - (start + r * PACK_ROWS) - pack_iota).astype(I16)
                klo_ref[rows, :] = jnp.where(tied, rev, jnp.int16(I16_MIN))
            return carry

        lax.fori_loop(0, n_pairs, reversed_positions, 0)
        base = jnp.full((1, Q_TILE), I16_MAX + 1 - 2 ** pos_bits, I32)
        last_rev, _ = refine16(klo_ref, need, base, zero, pos_bits)
        tsel_ref[...] = jnp.where(real, I16_MAX - last_rev, jnp.int32(-1))

    tsel = tsel_ref[...]

    scale = ATT_HEAD_DIM ** -0.5 * LOG2_E
    for h in range(N_ATT_HEADS):
        qh = q_ref[:, h * ATT_HEAD_DIM:(h + 1) * ATT_HEAD_DIM]
        qa = _dot_nt(wuk_ref[h], qh) * scale
        qabs_ref[:, h * Q_TILE:(h + 1) * Q_TILE] = qa.astype(BF16)

    width = N_ATT_HEADS * Q_TILE
    acc_ref[...] = jnp.zeros_like(acc_ref)

    def probabilities(c, slot, m):
        start = pl.multiple_of(c * KEY_CHUNK, KEY_CHUNK)
        kv = ckv_ref[pl.ds(start, KEY_CHUNK), :]
        kk = keys_ref[pl.ds(start, KEY_CHUNK), :]
        pos = start + row_iota
        bias = jnp.where(
            kk > thr, 0.0,
            jnp.where(kk == thr, jnp.where(pos <= tsel, 0.0, NEG_BIG), NEG_BIG))
        m_out, alphas = [], []
        for grp in range(width // MXU_WIDTH):
            lo = grp * MXU_WIDTH
            st = _dot(kv, qabs_ref[:, lo:lo + MXU_WIDTH])
            for hh in range(MXU_WIDTH // Q_TILE):
                a, b = lo + hh * Q_TILE, lo + (hh + 1) * Q_TILE
                s_h = st[:, hh * Q_TILE:(hh + 1) * Q_TILE] + bias
                m_old = m[:, a:b]
                m_new = jnp.maximum(m_old, jnp.max(s_h, axis=0, keepdims=True))
                alpha = jnp.exp2(m_old - m_new)
                p = jnp.exp2(s_h - m_new)
                m_out.append(m_new)
                alphas.append(alpha)
                p_ref[slot, :, a:b] = p.astype(BF16)
        cat = lambda xs: jnp.concatenate(xs, axis=1)
        return cat(m_out), cat(alphas)

    def accumulate(c, slot, alpha):
        kvt = ckvt_ref[c]
        for grp in range(width // MXU_WIDTH):
            lo, hi = grp * MXU_WIDTH, (grp + 1) * MXU_WIDTH
            pv = _dot(kvt, p_ref[slot, :, lo:hi])
            acc_ref[:, lo:hi] = acc_ref[:, lo:hi] * alpha[:, lo:hi] + pv

    p_ref[1] = jnp.zeros((KEY_CHUNK, width), BF16)

    def attn_pair(i, carry):
        m, alpha_prev = carry
        c0 = 2 * i
        m, alpha0 = probabilities(c0, 0, m)
        accumulate(jnp.maximum(c0 - 1, 0), 1, alpha_prev)
        m, alpha1 = probabilities(c0 + 1, 1, m)
        accumulate(c0, 0, alpha0)
        return m, alpha1

    m0 = jnp.full((1, width), NEG_BIG, F32)
    ones = jnp.ones((1, width), F32)
    _, alpha_last = lax.fori_loop(0, n_pairs, attn_pair, (m0, ones))
    accumulate(2 * n_pairs - 1, 1, alpha_last)

    denom = acc_ref[KV_RANK:KV_RANK + 1, :]
    o_lat = (acc_ref[0:KV_RANK, :] * (1.0 / denom)).astype(BF16)
    for h in range(N_ATT_HEADS):
        ot = _dot(wuvt_ref[h], o_lat[:, h * Q_TILE:(h + 1) * Q_TILE])
        o_ref[:, h * ATT_HEAD_DIM:(h + 1) * ATT_HEAD_DIM] = ot.T.astype(BF16)


def _dsa_attention(proj_b, wt, ckv, ckvt, kidx, wuk, wuvt):
    b, s, _ = proj_b.shape
    topk = min(TOPK_MAX, s // 4)
    assert s % (2 * KEY_CHUNK) == 0 and s <= I16_MAX + 1
    width = N_ATT_HEADS * Q_TILE
    return pl.pallas_call(
        functools.partial(_dsa_kernel, topk=topk, pos_bits=(s - 1).bit_length()),
        grid=(b, s // Q_TILE),
        in_specs=[
            pl.BlockSpec((None, Q_TILE, ATT_WIDTH),
                         lambda bi, i: (bi, i, PB_Q // ATT_WIDTH)),
            pl.BlockSpec((None, Q_TILE, IDX_HEADS * IDX_DIM),
                         lambda bi, i: (bi, i, PB_QIDX // (IDX_HEADS * IDX_DIM))),
            pl.BlockSpec((None, IDX_HEADS, Q_TILE), lambda bi, i: (bi, 0, i)),
            pl.BlockSpec((None, s, KV_RANK), lambda bi, i: (bi, 0, 0)),
            pl.BlockSpec((None, s // KEY_CHUNK, CKVT_ROWS, KEY_CHUNK),
                         lambda bi, i: (bi, 0, 0, 0)),
            pl.BlockSpec((None, s, IDX_DIM), lambda bi, i: (bi, 0, 0)),
            pl.BlockSpec((N_ATT_HEADS, KV_RANK, ATT_HEAD_DIM),
                         lambda bi, i: (0, 0, 0)),
            pl.BlockSpec((N_ATT_HEADS, ATT_HEAD_DIM, KV_RANK),
                         lambda bi, i: (0, 0, 0)),
        ],
        out_specs=pl.BlockSpec((None, Q_TILE, ATT_WIDTH), lambda bi, i: (bi, i, 0)),
        out_shape=jax.ShapeDtypeStruct((b, s, ATT_WIDTH), BF16),
        scratch_shapes=[
            pltpu.VMEM((s, Q_TILE), I32),
            pltpu.VMEM((s, Q_TILE), I16),
            pltpu.VMEM((s, Q_TILE), I16),
            pltpu.VMEM((s, Q_TILE), I16),
            pltpu.VMEM((IDX_HEADS * Q_TILE, IDX_DIM), BF16),
            pltpu.VMEM((KV_RANK, width), BF16),
            pltpu.VMEM((CKVT_ROWS, width), F32),
            pltpu.VMEM((1, Q_TILE), I32),
            pltpu.VMEM((2, KEY_CHUNK, width), BF16),
        ],
        compiler_params=_params("arbitrary", "arbitrary"),
        name="dsa_attention",
    )(proj_b, proj_b, wt, ckv, ckvt, kidx, wuk, wuvt)


def _ssd_kernel(z_ref, xs_ref, bm_ref, cm_ref, dt_ref, shift_ref,
                cwx_ref, cwb_ref, cwc_ref, cbx_ref, cbb_ref, cbc_ref,
                dtb_ref, alog_ref, dskip_ref, ng_ref, y_ref,
                tx_ref, tb_ref, tc_ref, state_ref, yacc_ref, *, chunk):
    ci = pl.program_id(1)
    taps = CONV_WIDTH - 1

    @pl.when(ci == 0)
    def _():
        tx_ref[...] = jnp.zeros_like(tx_ref)
        tb_ref[...] = jnp.zeros_like(tb_ref)
        tc_ref[...] = jnp.zeros_like(tc_ref)
        state_ref[...] = jnp.zeros_like(state_ref)

    def conv_silu(in_ref, tail_ref, w_ref, b_ref):
        cur = in_ref[...]
        shifted = _dot(shift_ref[...], cur)
        cur_f = cur.astype(F32)
        acc = b_ref[...] + w_ref[taps:taps + 1, :] * cur_f
        for k in range(taps):
            acc = acc + w_ref[k:k + 1, :] * shifted[k * chunk:(k + 1) * chunk]
        tail = tail_ref[...]
        row = lax.broadcasted_iota(I32, tail.shape, 0)
        fix = jnp.zeros_like(tail)
        for k in range(taps):
            back = taps - k
            fix = fix + jnp.where(row < back,
                                  w_ref[k:k + 1, :] * pltpu.roll(tail, back, 0), 0.0)
        acc = jnp.concatenate([acc[:SUBLANES] + fix, acc[SUBLANES:]], axis=0)
        tail_ref[...] = cur_f[chunk - SUBLANES:]
        return _silu(acc)

    xs = conv_silu(xs_ref, tx_ref, cwx_ref, cbx_ref)
    bm = conv_silu(bm_ref, tb_ref, cwb_ref, cbb_ref)
    cm = conv_silu(cm_ref, tc_ref, cwc_ref, cbc_ref)
    xs_b = xs.astype(BF16)

    dt = jax.nn.softplus(dt_ref[...] + dtb_ref[...])
    a2 = -jnp.exp(alog_ref[...]) * LOG2_E
    r_io = lax.broadcasted_iota(I32, (chunk, chunk), 0)
    c_io = lax.broadcasted_iota(I32, (chunk, chunk), 1)
    tril = r_io >= c_io
    a_cum = jnp.dot(tril.astype(F32), dt * a2, precision=lax.Precision.HIGHEST,
                    preferred_element_type=F32)
    total = a_cum[chunk - 1:chunk, :]
    to_end_t = (jnp.exp2(total - a_cum) * dt).T
    src_t = (a_cum - jnp.log2(dt)).T
    first = lax.broadcasted_iota(I32, (1, LANES), 1) < SSM_HEAD_DIM

    for g in range(N_SSM_GROUPS):
        bg = bm[:, g * D_STATE:(g + 1) * D_STATE]
        cg = cm[:, g * D_STATE:(g + 1) * D_STATE]
        cb = _dot_nt(cg.astype(BF16), bg.astype(BF16))
        bg_t = bg.T
        for pair in range(HEADS_PER_GROUP // 2):
            h0 = g * HEADS_PER_GROUP + 2 * pair
            lo = h0 * SSM_HEAD_DIM
            lhs, bgw, e_last = [], [], []
            for h in (h0, h0 + 1):
                col = jnp.broadcast_to(a_cum[:, h:h + 1], (chunk, chunk))
                decay = jnp.exp2(jnp.where(tril, col - src_t[h:h + 1, :], -jnp.inf))
                ecol = jnp.exp2(col)
                lhs.append(jnp.concatenate(
                    [(cb * decay).astype(BF16), (cg * ecol).astype(BF16)], axis=1))
                bgw.append((bg_t * to_end_t[h:h + 1, :]).astype(BF16))
                e_last.append(ecol[chunk - 1:chunk, :])
            xs_pair = xs_b[:, lo:lo + LANES]
            prev = state_ref[:, lo:lo + LANES]
            rhs = jnp.concatenate([xs_pair, prev.astype(BF16)], axis=0)
            r = _dot(jnp.concatenate(lhs, axis=0), rhs)
            yacc_ref[:, lo:lo + LANES] = jnp.where(first, r[:chunk], r[chunk:])
            nw = _dot(jnp.concatenate(bgw, axis=0), xs_pair)
            keep = jnp.where(first, e_last[0], e_last[1])
            state_ref[:, lo:lo + LANES] = prev * keep + jnp.where(
                first, nw[:D_STATE], nw[D_STATE:])

    y = yacc_ref[...] + xs * dskip_ref[...]
    zf = z_ref[...].astype(F32)
    y = y * _silu(zf)
    gw = D_INNER // N_SSM_GROUPS
    for g in range(N_SSM_GROUPS):
        yg = y[:, g * gw:(g + 1) * gw]
        yn = yg * _rms_scale(yg) * ng_ref[:, g * gw:(g + 1) * gw]
        y_ref[:, g * gw:(g + 1) * gw] = yn.astype(BF16)


def _ssd(proj_b, proj_f, conv_w, conv_b, dt_bias, a_log, d_skip, ssm_norm_g):
    b, s, _ = proj_b.shape
    chunk = SSD_CHUNK
    assert chunk == D_STATE == LANES and s % chunk == 0
    pad = LANES - N_SSM_HEADS
    taps = CONV_WIDTH - 1
    t_io = np.arange(chunk)
    shift = np.concatenate(
        [(t_io[:, None] - (taps - k) == t_io[None, :]) for k in range(taps)], axis=0)
    shift = jnp.asarray(shift, BF16)
    cw_x, cw_b, cw_c = (conv_w[:, :D_INNER], conv_w[:, D_INNER:D_INNER + BC_WIDTH],
                        conv_w[:, D_INNER + BC_WIDTH:])
    cb = conv_b.reshape(1, -1)
    cb_x, cb_b, cb_c = (cb[:, :D_INNER], cb[:, D_INNER:D_INNER + BC_WIDTH],
                        cb[:, D_INNER + BC_WIDTH:])
    dtb = jnp.pad(dt_bias, (0, pad)).reshape(1, LANES)
    alog = jnp.pad(a_log, (0, pad)).reshape(1, LANES)
    dskip = jnp.repeat(d_skip, SSM_HEAD_DIM).reshape(1, D_INNER)
    const = lambda bi, i: (0, 0)
    return pl.pallas_call(
        functools.partial(_ssd_kernel, chunk=chunk),
        grid=(b, s // chunk),
        in_specs=[
            pl.BlockSpec((None, chunk, D_INNER), lambda bi, i: (bi, i, PB_Z // D_INNER)),
            pl.BlockSpec((None, chunk, D_INNER), lambda bi, i: (bi, i, PB_XS // D_INNER)),
            pl.BlockSpec((None, chunk, BC_WIDTH), lambda bi, i: (bi, i, PB_BM // BC_WIDTH)),
            pl.BlockSpec((None, chunk, BC_WIDTH), lambda bi, i: (bi, i, PB_CM // BC_WIDTH)),
            pl.BlockSpec((None, chunk, LANES), lambda bi, i: (bi, i, PF_DT // LANES)),
            pl.BlockSpec((taps * chunk, chunk), const),
            pl.BlockSpec((CONV_WIDTH, D_INNER), const),
            pl.BlockSpec((CONV_WIDTH, BC_WIDTH), const),
            pl.BlockSpec((CONV_WIDTH, BC_WIDTH), const),
            pl.BlockSpec((1, D_INNER), const),
            pl.BlockSpec((1, BC_WIDTH), const),
            pl.BlockSpec((1, BC_WIDTH), const),
            pl.BlockSpec((1, LANES), const),
            pl.BlockSpec((1, LANES), const),
            pl.BlockSpec((1, D_INNER), const),
            pl.BlockSpec((1, D_INNER), const),
        ],
        out_specs=pl.BlockSpec((None, chunk, D_INNER), lambda bi, i: (bi, i, 0)),
        out_shape=jax.ShapeDtypeStruct((b, s, D_INNER), BF16),
        scratch_shapes=[
            pltpu.VMEM((SUBLANES, D_INNER), F32),
            pltpu.VMEM((SUBLANES, BC_WIDTH), F32),
            pltpu.VMEM((SUBLANES, BC_WIDTH), F32),
            pltpu.VMEM((D_STATE, D_INNER), F32),
            pltpu.VMEM((chunk, D_INNER), F32),
        ],
        compiler_params=_params("arbitrary", "arbitrary"),
        name="ssd_mixer",
    )(proj_b, proj_b, proj_b, proj_b, proj_f, shift, cw_x, cw_b, cw_c, cb_x, cb_b, cb_c,
      dtb, alog, dskip, ssm_norm_g.reshape(1, D_INNER))


def _mix_out_kernel(att_ref, y_ref, ga_ref, gs_ref, x_ref, gate_ref, g_ref,
                    woa_ref, wos_ref, wout_ref, o_ref):
    ba = _dot(att_ref[...], woa_ref[...])
    bs = _dot(y_ref[...], wos_ref[...])
    merged = (jax.nn.sigmoid(ga_ref[...].astype(F32)) * ba
              + jax.nn.sigmoid(gs_ref[...].astype(F32)) * bs)
    out = _dot(merged.astype(BF16), wout_ref[...])
    normed = out * _rms_scale(out) * g_ref[...]
    o_ref[...] = x_ref[...] + gate_ref[...] * normed


def _mix_out(att, y, proj_b, x, gate, g, woa, wos, wout):
    b, s, d = x.shape
    tm = min(OUT_TM, s)
    const = lambda bi, i: (0, 0)
    return pl.pallas_call(
        _mix_out_kernel,
        grid=(b, s // tm),
        in_specs=[
            pl.BlockSpec((None, tm, ATT_WIDTH), lambda bi, i: (bi, i, 0)),
            pl.BlockSpec((None, tm, D_INNER), lambda bi, i: (bi, i, 0)),
            pl.BlockSpec((None, tm, d), lambda bi, i: (bi, i, PB_GATT // d)),
            pl.BlockSpec((None, tm, d), lambda bi, i: (bi, i, PB_GSSD // d)),
            pl.BlockSpec((None, tm, d), lambda bi, i: (bi, i, 0)),
            pl.BlockSpec((None, 1, d), lambda bi, i: (bi, 0, 0)),
            pl.BlockSpec((1, d), const),
            pl.BlockSpec((ATT_WIDTH, d), const),
            pl.BlockSpec((D_INNER, d), const),
            pl.BlockSpec((d, d), const),
        ],
        out_specs=pl.BlockSpec((None, tm, d), lambda bi, i: (bi, i, 0)),
        out_shape=jax.ShapeDtypeStruct((b, s, d), F32),
        compiler_params=_params("arbitrary", "arbitrary"),
        name="mix_out",
    )(att, y, proj_b, proj_b, x, gate, g, woa, wos, wout)


def _mlp_kernel(x_ref, g_in_ref, sc_ref, sh_ref, wup_ref, wdn_ref, gate_ref,
                g_out_ref, o_ref, h_ref, acc_ref):
    j = pl.program_id(2)

    @pl.when(j == 0)
    def _():
        x = x_ref[...]
        h = x * _rms_scale(x) * g_in_ref[...]
        h = h * (1.0 + sc_ref[...]) + sh_ref[...]
        h_ref[...] = h.astype(BF16)
        acc_ref[...] = jnp.zeros_like(acc_ref)

    up = jnp.maximum(_dot(h_ref[...], wup_ref[...]), 0.0)
    acc_ref[...] += _dot((up * up).astype(BF16), wdn_ref[...])

    @pl.when(j == pl.num_programs(2) - 1)
    def _():
        y = acc_ref[...]
        normed = y * _rms_scale(y) * g_out_ref[...]
        o_ref[...] = x_ref[...] + gate_ref[...] * normed


def _mlp(x, g_in, sc, sh, wup, wdn, gate, g_out):
    b, s, d = x.shape
    ff = wup.shape[1]
    tm = min(MLP_TM, s)
    per_b = lambda bi, i, j: (bi, 0, 0)
    const = lambda bi, i, j: (0, 0)
    return pl.pallas_call(
        _mlp_kernel,
        grid=(b, s // tm, ff // MLP_TF),
        in_specs=[
            pl.BlockSpec((None, tm, d), lambda bi, i, j: (bi, i, 0)),
            pl.BlockSpec((1, d), const),
            pl.BlockSpec((None, 1, d), per_b),
            pl.BlockSpec((None, 1, d), per_b),
            pl.BlockSpec((d, MLP_TF), lambda bi, i, j: (0, j)),
            pl.BlockSpec((MLP_TF, d), lambda bi, i, j: (j, 0)),
            pl.BlockSpec((None, 1, d), per_b),
            pl.BlockSpec((1, d), const),
        ],
        out_specs=pl.BlockSpec((None, tm, d), lambda bi, i, j: (bi, i, 0)),
        out_shape=jax.ShapeDtypeStruct((b, s, d), F32),
        scratch_shapes=[pltpu.VMEM((tm, d), BF16), pltpu.VMEM((tm, d), F32)],
        compiler_params=_params("arbitrary", "arbitrary", "arbitrary"),
        name="mlp",
    )(x, g_in, sc, sh, wup, wdn, gate, g_out)


def _split_offsets():
    sizes = (ATT_WIDTH, KV_RANK, IDX_HEADS * IDX_DIM, IDX_DIM, IDX_HEADS,
             D_INNER, D_INNER, BC_WIDTH, BC_WIDTH, N_SSM_HEADS, ATT_WIDTH, ATT_WIDTH)
    names = ("q", "kv", "qidx", "kidx", "widx", "z", "xs", "bm", "cm", "dt",
             "gatt", "gssd")
    offs = np.concatenate([[0], np.cumsum(sizes)])
    return {n: (int(offs[i]), int(offs[i + 1])) for i, n in enumerate(names)}


def _pack_w_in(w_in):
    sl = _split_offsets()
    col = lambda n: w_in[..., sl[n][0]:sl[n][1]]
    wb = jnp.concatenate([col(n) for n in
                          ("z", "xs", "q", "gatt", "gssd", "bm", "cm", "qidx")],
                         axis=-1).astype(BF16)
    lead = w_in.shape[:-1]
    zeros = lambda n: jnp.zeros(lead + (n,), w_in.dtype)
    wf = jnp.concatenate([
        col("kv"), col("kidx"), col("widx"), zeros(LANES - IDX_DIM - IDX_HEADS),
        col("dt"), zeros(LANES - N_SSM_HEADS)], axis=-1).astype(BF16)
    return wb, wf


def kernel(x, c, ada_w, ada_b, norm_g, w_in, kv_norm_g, kidx_norm_g, w_uk, w_uv,
           conv_w, conv_b, dt_bias, a_log, d_skip, ssm_norm_g, w_o_att, w_o_ssd,
           w_out, w_up, w_down):
    depth = ada_w.shape[0]
    d = x.shape[-1]
    mod = _modulation(c, ada_w, ada_b)
    wb_all, wf_all = _pack_w_in(w_in)
    wuk_all = jnp.transpose(w_uk, (0, 2, 1, 3)).astype(BF16)
    wuvt_all = jnp.transpose(w_uv, (0, 2, 3, 1)).astype(BF16)
    woa_all = w_o_att.astype(BF16)
    wos_all = w_o_ssd.astype(BF16)
    wout_all = w_out.astype(BF16)
    wup_all = w_up.astype(BF16)
    wdn_all = w_down.astype(BF16)

    for layer in range(depth):
        sh_m, sc_m, gt_m, sh_f, sc_f, gt_f = (mod[layer, :, k] for k in range(N_MOD))
        ng = lambda k: norm_g[layer, k].reshape(1, d)
        proj_b, proj_f = _input_projection(x, ng(0), sc_m, sh_m,
                                           wb_all[layer], wf_all[layer])
        ckv, ckvt, kidx, wt = _latent_prep(
            proj_f, kv_norm_g[layer].reshape(1, KV_RANK),
            kidx_norm_g[layer].reshape(1, IDX_DIM))
        att = _dsa_attention(proj_b, wt, ckv, ckvt, kidx,
                             wuk_all[layer], wuvt_all[layer])
        y = _ssd(proj_b, proj_f, conv_w[layer], conv_b[layer], dt_bias[layer],
                 a_log[layer], d_skip[layer], ssm_norm_g[layer])
        x = _mix_out(att, y, proj_b, x, gt_m, ng(1), woa_all[layer],
                     wos_all[layer], wout_all[layer])
        x = _mlp(x, ng(2), sc_f, sh_f, wup_all[layer], wdn_all[layer], gt_f, ng(3))
    return x
```

```python
import functools

import jax
import jax.numpy as jnp
import numpy as np
from jax import lax
from jax.experimental import pallas as pl
from jax.experimental.pallas import tpu as pltpu

F32 = jnp.float32
BF16 = jnp.bfloat16
I32 = jnp.int32
I16 = jnp.int16

N_ATT_HEADS = 8
ATT_HEAD_DIM = 128
ATT_WIDTH = N_ATT_HEADS * ATT_HEAD_DIM
KV_RANK = 256
IDX_HEADS = 8
IDX_DIM = 64
TOPK_MAX = 256
D_STATE = 128
SSM_HEAD_DIM = 64
N_SSM_GROUPS = 4
HEADS_PER_GROUP = 8
N_SSM_HEADS = N_SSM_GROUPS * HEADS_PER_GROUP
D_INNER = N_SSM_HEADS * SSM_HEAD_DIM
BC_WIDTH = N_SSM_GROUPS * D_STATE
CONV_WIDTH = 4
N_MOD = 6
EPS = 1e-6

LANES = 128
SUBLANES = 8
VMEM_LIMIT_BYTES = 52 * 1024 * 1024

MXU_WIDTH = 256
Q_TILE = 256
KEY_CHUNK = 256
CKVT_ROWS = KV_RANK + 16
SSD_CHUNK = 128
INPROJ_TM = 2048
INPROJ_TN = 512
OUT_TM = 512
MLP_TM = 1024
MLP_TF = 1024
MOD_TN = 1536

INT_MIN = -2147483648
I16_MIN = -32768
I16_MAX = 32767
HALF_BIAS = 32768
PACK_ROWS = 16
LOG2_E = 1.4426950408889634
COUNT_ACCS = 8
SELECT_GROUP = 5
NEG_BIG = -1e30

PB_Z, PB_XS, PB_Q, PB_GATT, PB_GSSD, PB_BM, PB_CM, PB_QIDX = (
    0, 2048, 4096, 5120, 6144, 7168, 7680, 8192)
PB_WIDTH = 8704
PF_KV, PF_KIDX, PF_DT = 0, 256, 384
PF_WIDTH = 512


def _params(*sem):
    return pltpu.CompilerParams(dimension_semantics=sem,
                                vmem_limit_bytes=VMEM_LIMIT_BYTES)


def _dot(a, b):
    return jnp.dot(a, b, preferred_element_type=F32)


def _dot_nt(a, b):
    return lax.dot_general(a, b, (((1,), (1,)), ((), ())),
                           preferred_element_type=F32)


def _silu(x):
    return x * jax.nn.sigmoid(x)


def _rms_scale(x):
    return lax.rsqrt(jnp.mean(x * x, axis=-1, keepdims=True) + EPS)


def _mod_kernel(c_ref, w_ref, b_ref, o_ref):
    c = c_ref[...]
    c_act = (c * jax.nn.sigmoid(c)).astype(BF16)
    o_ref[...] = _dot(c_act, w_ref[...].astype(BF16)) + b_ref[...]


def _modulation(c, ada_w, ada_b):
    n_layers, d, n = ada_w.shape
    b = c.shape[0]
    rows = ((b + SUBLANES - 1) // SUBLANES) * SUBLANES
    c_pad = jnp.zeros((rows, d), F32).at[:b].set(c)
    out = pl.pallas_call(
        _mod_kernel,
        grid=(n_layers, n // MOD_TN),
        in_specs=[
            pl.BlockSpec((rows, d), lambda l, j: (0, 0)),
            pl.BlockSpec((None, d, MOD_TN), lambda l, j: (l, 0, j)),
            pl.BlockSpec((None, 1, MOD_TN), lambda l, j: (l, 0, j)),
        ],
        out_specs=pl.BlockSpec((None, rows, MOD_TN), lambda l, j: (l, 0, j)),
        out_shape=jax.ShapeDtypeStruct((n_layers, rows, n), F32),
        compiler_params=_params("arbitrary", "arbitrary"),
        name="adaln_mod",
    )(c_pad, ada_w, ada_b.reshape(n_layers, 1, n))
    return out[:, :b].reshape(n_layers, b, N_MOD, 1, d)


def _inproj_kernel(x_ref, g_ref, sc_ref, sh_ref, wb_ref, wf_ref,
                   ob_ref, of_ref, h_ref, *, n_bf16_tiles):
    j = pl.program_id(2)

    @pl.when(j == 0)
    def _():
        x = x_ref[...]
        h = x * _rms_scale(x) * g_ref[...]
        h = h * (1.0 + sc_ref[...]) + sh_ref[...]
        h_ref[...] = h.astype(BF16)

    @pl.when(j < n_bf16_tiles)
    def _():
        ob_ref[...] = _dot(h_ref[...], wb_ref[...]).astype(BF16)

    @pl.when(j == n_bf16_tiles)
    def _():
        of_ref[...] = _dot(h_ref[...], wf_ref[...])


def _input_projection(x, g, sc, sh, wb, wf):
    b, s, d = x.shape
    tm = min(INPROJ_TM, s)
    nj = PB_WIDTH // INPROJ_TN
    last = nj - 1
    return pl.pallas_call(
        functools.partial(_inproj_kernel, n_bf16_tiles=nj),
        grid=(b, s // tm, nj + 1),
        in_specs=[
            pl.BlockSpec((None, tm, d), lambda bi, i, j: (bi, i, 0)),
            pl.BlockSpec((1, d), lambda bi, i, j: (0, 0)),
            pl.BlockSpec((None, 1, d), lambda bi, i, j: (bi, 0, 0)),
            pl.BlockSpec((None, 1, d), lambda bi, i, j: (bi, 0, 0)),
            pl.BlockSpec((d, INPROJ_TN),
                         lambda bi, i, j: (0, jnp.minimum(j, last))),
            pl.BlockSpec((d, PF_WIDTH), lambda bi, i, j: (0, 0)),
        ],
        out_specs=[
            pl.BlockSpec((None, tm, INPROJ_TN),
                         lambda bi, i, j: (bi, i, jnp.minimum(j, last))),
            pl.BlockSpec((None, tm, PF_WIDTH), lambda bi, i, j: (bi, i, 0)),
        ],
        out_shape=[
            jax.ShapeDtypeStruct((b, s, PB_WIDTH), BF16),
            jax.ShapeDtypeStruct((b, s, PF_WIDTH), F32),
        ],
        scratch_shapes=[pltpu.VMEM((tm, d), BF16)],
        compiler_params=_params("arbitrary", "arbitrary", "arbitrary"),
        name="norm_inproj",
    )(x, g, sc, sh, wb, wf)


def _prep_kernel(kv_ref, kw_ref, gkv_ref, gk_ref, ckv_ref, ckvt_ref,
                 kidx_ref, wt_ref, *, n_chunks):
    kv = kv_ref[...]
    ckv = kv * _rms_scale(kv) * gkv_ref[...]
    ckv_ref[...] = ckv.astype(BF16)
    extra_row = lax.broadcasted_iota(I32, (CKVT_ROWS - KV_RANK, KEY_CHUNK), 0)
    extra = jnp.where(extra_row == 0, 1.0, 0.0).astype(BF16)
    for c in range(n_chunks):
        blk = ckv[c * KEY_CHUNK:(c + 1) * KEY_CHUNK, :]
        ckvt_ref[c] = jnp.concatenate([blk.T.astype(BF16), extra], axis=0)
    kw = kw_ref[...]
    k = kw[:, :IDX_DIM]
    kn = k * _rms_scale(k) * gk_ref[...]
    kidx_ref[...] = kn.astype(BF16)
    kwt = kw.T
    wt_ref[...] = kwt[IDX_DIM:IDX_DIM + IDX_HEADS, :] * (IDX_HEADS ** -0.5)


def _latent_prep(proj_f, kv_norm_g, kidx_norm_g):
    b, s, _ = proj_f.shape
    ts = min(1024, s)
    n_chunks = ts // KEY_CHUNK
    return pl.pallas_call(
        functools.partial(_prep_kernel, n_chunks=n_chunks),
        grid=(b, s // ts),
        in_specs=[
            pl.BlockSpec((None, ts, KV_RANK),
                         lambda bi, i: (bi, i, PF_KV // KV_RANK)),
            pl.BlockSpec((None, ts, LANES),
                         lambda bi, i: (bi, i, PF_KIDX // LANES)),
            pl.BlockSpec((1, KV_RANK), lambda bi, i: (0, 0)),
            pl.BlockSpec((1, IDX_DIM), lambda bi, i: (0, 0)),
        ],
        out_specs=[
            pl.BlockSpec((None, ts, KV_RANK), lambda bi, i: (bi, i, 0)),
            pl.BlockSpec((None, n_chunks, CKVT_ROWS, KEY_CHUNK),
                         lambda bi, i: (bi, i, 0, 0)),
            pl.BlockSpec((None, ts, IDX_DIM), lambda bi, i: (bi, i, 0)),
            pl.BlockSpec((None, IDX_HEADS, ts), lambda bi, i: (bi, 0, i)),
        ],
        out_shape=[
            jax.ShapeDtypeStruct((b, s, KV_RANK), BF16),
            jax.ShapeDtypeStruct((b, s // KEY_CHUNK, CKVT_ROWS, KEY_CHUNK), BF16),
            jax.ShapeDtypeStruct((b, s, IDX_DIM), BF16),
            jax.ShapeDtypeStruct((b, IDX_HEADS, s), F32),
        ],
        compiler_params=_params("arbitrary", "arbitrary"),
        name="latent_prep",
    )(proj_f, proj_f, kv_norm_g, kidx_norm_g)


def _dsa_kernel(q_ref, qidx_ref, wt_ref, ckv_ref, ckvt_ref, kidx_ref,
                wuk_ref, wuvt_ref, o_ref,
                keys_ref, khi_ref, klo_ref, lom_ref, qall_ref, qabs_ref, acc_ref,
                tsel_ref, p_ref,
                *, topk, pos_bits):
    qb = pl.program_id(1)
    n_chunks = (qb * Q_TILE + Q_TILE + KEY_CHUNK - 1) // KEY_CHUNK
    q_pos = qb * Q_TILE + lax.broadcasted_iota(I32, (KEY_CHUNK, Q_TILE), 1)
    row_iota = lax.broadcasted_iota(I32, (KEY_CHUNK, Q_TILE), 0)

    for h in range(IDX_HEADS):
        qall_ref[h * Q_TILE:(h + 1) * Q_TILE, :] = (
            qidx_ref[:, h * IDX_DIM:(h + 1) * IDX_DIM])

    n_full = n_chunks // 2
    n_pairs = (n_chunks + 1) // 2
    pair_rows = 2 * KEY_CHUNK

    def score_chunk(c):
        start = pl.multiple_of(c * KEY_CHUNK, KEY_CHUNK)
        ks = kidx_ref[pl.ds(start, KEY_CHUNK), :]
        logits = _dot_nt(ks, qall_ref[...])
        score = jnp.zeros((KEY_CHUNK, Q_TILE), F32)
        for h in range(IDX_HEADS):
            lg = logits[:, h * Q_TILE:(h + 1) * Q_TILE]
            score = score + wt_ref[h:h + 1, :] * jnp.maximum(lg, 0.0)
        bits = pltpu.bitcast(score, I32)
        key = bits ^ ((bits >> 31) & 0x7FFFFFFF)
        key = jnp.where(start + row_iota > q_pos, INT_MIN, key)
        keys_ref[pl.ds(start, KEY_CHUNK), :] = key
        khi_ref[pl.ds(start, KEY_CHUNK), :] = (key >> 16).astype(I16)
        klo_ref[pl.ds(start, KEY_CHUNK), :] = ((key & 0xFFFF) - HALF_BIAS).astype(I16)

    def score_pair(i, carry):
        score_chunk(2 * i)
        score_chunk(2 * i + 1)
        return carry

    lax.fori_loop(0, n_full, score_pair, 0)

    @pl.when(n_chunks % 2 == 1)
    def _():
        score_chunk(n_chunks - 1)
        pad = pl.multiple_of(n_chunks * KEY_CHUNK, KEY_CHUNK)
        never = jnp.full((KEY_CHUNK, Q_TILE), I16_MIN, I16)
        khi_ref[pl.ds(pad, KEY_CHUNK), :] = never
        klo_ref[pl.ds(pad, KEY_CHUNK), :] = never

    zero = jnp.zeros((1, Q_TILE), I32)

    def count16(ref, t):
        t16 = jnp.broadcast_to(t, (PACK_ROWS, Q_TILE)).astype(I16)

        def body(c, accs):
            start = pl.multiple_of(c * pair_rows, pair_rows)
            blk = ref[pl.ds(start, pair_rows), :]
            accs = list(accs)
            for r in range(pair_rows // PACK_ROWS):
                hit = jnp.where(blk[r * PACK_ROWS:(r + 1) * PACK_ROWS] >= t16,
                                jnp.int16(1), jnp.int16(0))
                accs[r % COUNT_ACCS] = accs[r % COUNT_ACCS] + hit
            return tuple(accs)
        zeros16 = jnp.zeros((PACK_ROWS, Q_TILE), I16)
        accs = lax.fori_loop(0, n_pairs, body, (zeros16,) * COUNT_ACCS)
        total = accs[0].astype(I32)
        for k in range(1, COUNT_ACCS):
            total = total + accs[k].astype(I32)
        return jnp.sum(total, axis=0, keepdims=True)

    def refine16(ref, target, cand, cand_cnt, n_bits, live=None):
        def bit_step(i, carry):
            cand, cand_cnt = carry
            trial = cand | (jnp.int32(1) << (n_bits - 1 - i))
            cnt = count16(ref, trial)
            keep = cnt >= target
            return jnp.where(keep, trial, cand), jnp.where(keep, cnt, cand_cnt)

        if live is None:
            return lax.fori_loop(0, n_bits, bit_step, (cand, cand_cnt))

        def unsettled(cnt):
            return jnp.max(jnp.where(live & (cnt != target), 1, 0))

        def group_step(state):
            g, cand, cand_cnt, _ = state
            for j in range(SELECT_GROUP):
                cand, cand_cnt = bit_step(g * SELECT_GROUP + j, (cand, cand_cnt))
            return g + 1, cand, cand_cnt, unsettled(cand_cnt)

        _, cand, cand_cnt, _ = lax.while_loop(
            lambda st: (st[0] < n_bits // SELECT_GROUP) & (st[3] > 0),
            group_step, (jnp.int32(0), cand, cand_cnt, unsettled(cand_cnt)))
        return cand, cand_cnt

    def select16(ref, target, floor_cnt, live=None):
        cnt0 = count16(ref, zero)
        ok = cnt0 >= target
        cand = jnp.where(ok, zero, jnp.full((1, Q_TILE), I16_MIN, I32))
        return refine16(ref, target, cand, jnp.where(ok, cnt0, floor_cnt), 15, live)

    thr_hi, cnt_ge_hi = select16(khi_ref, topk, zero)
    cnt_above = count16(khi_ref, thr_hi + 1)
    real = thr_hi != I16_MIN

    thr_hi16 = jnp.broadcast_to(thr_hi, (PACK_ROWS, Q_TILE)).astype(I16)

    def mask_low(c, carry):
        start = pl.multiple_of(c * pair_rows, pair_rows)
        for r in range(pair_rows // PACK_ROWS):
            rows = pl.ds(start + r * PACK_ROWS, PACK_ROWS)
            lom_ref[rows, :] = jnp.where(khi_ref[rows, :] == thr_hi16,
                                         klo_ref[rows, :], jnp.int16(I16_MIN))
        return carry

    lax.fori_loop(0, n_pairs, mask_low, 0)
    thr_lo, cnt_lo = select16(lom_ref, topk - cnt_above, cnt_ge_hi - cnt_above,
                              live=real)

    thr = jnp.where(real, (thr_hi << 16) | (thr_lo + HALF_BIAS), INT_MIN)
    thr_cnt = cnt_above + cnt_lo

    tsel_ref[...] = jnp.where(real, jnp.int32(2 ** 30), jnp.int32(-1))
    n_tied = jnp.max(jnp.where(real & (thr_cnt > topk), 1, 0))

    @pl.when(n_tied > 0)
    def _():
        cnt_gt = cnt_above + jnp.where(
            thr_lo < I16_MAX, count16(lom_ref, thr_lo + 1), 0)
        need = topk - cnt_gt
        thr_lo16 = jnp.broadcast_to(thr_lo, (PACK_ROWS, Q_TILE)).astype(I16)
        pack_iota = lax.broadcasted_iota(I32, (PACK_ROWS, Q_TILE), 0)

        def reversed_positions(c, carry):
            start = pl.multiple_of(c * pair_rows, pair_rows)
            for r in range(pair_rows // PACK_ROWS):
                rows = pl.ds(start + r * PACK_ROWS, PACK_ROWS)
                tied = ((khi_ref[rows, :] == thr_hi16)
                        & (lom_ref[rows, :] == thr_lo16))
                rev = (I16_MAX - (start + r * PACK_ROWS) - pack_iota).astype(I16)
                klo_ref[rows, :] = jnp.where(tied, rev, jnp.int16(I16_MIN))
            return carry

        lax.fori_loop(0, n_pairs, reversed_positions, 0)
        base = jnp.full((1, Q_TILE), I16_MAX + 1 - 2 ** pos_bits, I32)
        last_rev, _ = refine16(klo_ref, need, base, zero, pos_bits)
        tsel_ref[...] = jnp.where(real, I16_MAX - last_rev, jnp.int32(-1))

    tsel = tsel_ref[...]

    scale = ATT_HEAD_DIM ** -0.5 * LOG2_E
    for h in range(N_ATT_HEADS):
        qh = q_ref[:, h * ATT_HEAD_DIM:(h + 1) * ATT_HEAD_DIM]
        qa = _dot_nt(wuk_ref[h], qh) * scale
        qabs_ref[:, h * Q_TILE:(h + 1) * Q_TILE] = qa.astype(BF16)

    width = N_ATT_HEADS * Q_TILE
    acc_ref[...] = jnp.zeros_like(acc_ref)

    def probabilities(c, slot, m):
        start = pl.multiple_of(c * KEY_CHUNK, KEY_CHUNK)
        kv = ckv_ref[pl.ds(start, KEY_CHUNK), :]
        kk = keys_ref[pl.ds(start, KEY_CHUNK), :]
        pos = start + row_iota
        bias = jnp.where(
            kk > thr, 0.0,
            jnp.where(kk == thr, jnp.where(pos <= tsel, 0.0, NEG_BIG), NEG_BIG))
        m_out, alphas = [], []
        for grp in range(width // MXU_WIDTH):
            lo = grp * MXU_WIDTH
            st = _dot(kv, qabs_ref[:, lo:lo + MXU_WIDTH])
            for hh in range(MXU_WIDTH // Q_TILE):
                a, b = lo + hh * Q_TILE, lo + (hh + 1) * Q_TILE
                s_h = st[:, hh * Q_TILE:(hh + 1) * Q_TILE] + bias
                m_old = m[:, a:b]
                m_new = jnp.maximum(m_old, jnp.max(s_h, axis=0, keepdims=True))
                alpha = jnp.exp2(m_old - m_new)
                p = jnp.exp2(s_h - m_new)
                m_out.append(m_new)
                alphas.append(alpha)
                p_ref[slot, :, a:b] = p.astype(BF16)
        cat = lambda xs: jnp.concatenate(xs, axis=1)
        return cat(m_out), cat(alphas)

    def accumulate(c, slot, alpha):
        kvt = ckvt_ref[c]
        for grp in range(width // MXU_WIDTH):
            lo, hi = grp * MXU_WIDTH, (grp + 1) * MXU_WIDTH
            pv = _dot(kvt, p_ref[slot, :, lo:hi])
            acc_ref[:, lo:hi] = acc_ref[:, lo:hi] * alpha[:, lo:hi] + pv

    p_ref[1] = jnp.zeros((KEY_CHUNK, width), BF16)

    def attn_pair(i, carry):
        m, alpha_prev = carry
        c0 = 2 * i
        m, alpha0 = probabilities(c0, 0, m)
        accumulate(jnp.maximum(c0 - 1, 0), 1, alpha_prev)
        m, alpha1 = probabilities(c0 + 1, 1, m)
        accumulate(c0, 0, alpha0)
        return m, alpha1

    m0 = jnp.full((1, width), NEG_BIG, F32)
    ones = jnp.ones((1, width), F32)
    m_last, alpha_last = lax.fori_loop(0, n_full, attn_pair, (m0, ones))

    @pl.when(n_chunks % 2 == 0)
    def _():
        accumulate(n_chunks - 1, 1, alpha_last)

    @pl.when(n_chunks % 2 == 1)
    def _():
        last = n_chunks - 1
        _, alpha0 = probabilities(last, 0, m_last)
        accumulate(jnp.maximum(last - 1, 0), 1, alpha_last)
        accumulate(last, 0, alpha0)

    denom = acc_ref[KV_RANK:KV_RANK + 1, :]
    o_lat = (acc_ref[0:KV_RANK, :] * (1.0 / denom)).astype(BF16)
    for h in range(N_ATT_HEADS):
        ot = _dot(wuvt_ref[h], o_lat[:, h * Q_TILE:(h + 1) * Q_TILE])
        o_ref[:, h * ATT_HEAD_DIM:(h + 1) * ATT_HEAD_DIM] = ot.T.astype(BF16)


def _dsa_attention(proj_b, wt, ckv, ckvt, kidx, wuk, wuvt):
    b, s, _ = proj_b.shape
    topk = min(TOPK_MAX, s // 4)
    assert s % (2 * KEY_CHUNK) == 0 and s <= I16_MAX + 1
    width = N_ATT_HEADS * Q_TILE
    return pl.pallas_call(
        functools.partial(_dsa_kernel, topk=topk, pos_bits=(s - 1).bit_length()),
        grid=(b, s // Q_TILE),
        in_specs=[
            pl.BlockSpec((None, Q_TILE, ATT_WIDTH),
                         lambda bi, i: (bi, i, PB_Q // ATT_WIDTH)),
            pl.BlockSpec((None, Q_TILE, IDX_HEADS * IDX_DIM),
                         lambda bi, i: (bi, i, PB_QIDX // (IDX_HEADS * IDX_DIM))),
            pl.BlockSpec((None, IDX_HEADS, Q_TILE), lambda bi, i: (bi, 0, i)),
            pl.BlockSpec((None, s, KV_RANK), lambda bi, i: (bi, 0, 0)),
            pl.BlockSpec((None, s // KEY_CHUNK, CKVT_ROWS, KEY_CHUNK),
                         lambda bi, i: (bi, 0, 0, 0)),
            pl.BlockSpec((None, s, IDX_DIM), lambda bi, i: (bi, 0, 0)),
            pl.BlockSpec((N_ATT_HEADS, KV_RANK, ATT_HEAD_DIM),
                         lambda bi, i: (0, 0, 0)),
            pl.BlockSpec((N_ATT_HEADS, ATT_HEAD_DIM, KV_RANK),
                         lambda bi, i: (0, 0, 0)),
        ],
        out_specs=pl.BlockSpec((None, Q_TILE, ATT_WIDTH), lambda bi, i: (bi, i, 0)),
        out_shape=jax.ShapeDtypeStruct((b, s, ATT_WIDTH), BF16),
        scratch_shapes=[
            pltpu.VMEM((s, Q_TILE), I32),
            pltpu.VMEM((s, Q_TILE), I16),
            pltpu.VMEM((s, Q_TILE), I16),
            pltpu.VMEM((s, Q_TILE), I16),
            pltpu.VMEM((IDX_HEADS * Q_TILE, IDX_DIM), BF16),
            pltpu.VMEM((KV_RANK, width), BF16),
            pltpu.VMEM((CKVT_ROWS, width), F32),
            pltpu.VMEM((1, Q_TILE), I32),
            pltpu.VMEM((2, KEY_CHUNK, width), BF16),
        ],
        compiler_params=_params("arbitrary", "arbitrary"),
        name="dsa_attention",
    )(proj_b, proj_b, wt, ckv, ckvt, kidx, wuk, wuvt)


def _ssd_kernel(z_ref, xs_ref, bm_ref, cm_ref, dt_ref, shift_ref,
                cwx_ref, cwb_ref, cwc_ref, cbx_ref, cbb_ref, cbc_ref,
                dtb_ref, alog_ref, dskip_ref, ng_ref, y_ref,
                tx_ref, tb_ref, tc_ref, state_ref, yacc_ref, *, chunk):
    ci = pl.program_id(1)
    taps = CONV_WIDTH - 1

    @pl.when(ci == 0)
    def _():
        tx_ref[...] = jnp.zeros_like(tx_ref)
        tb_ref[...] = jnp.zeros_like(tb_ref)
        tc_ref[...] = jnp.zeros_like(tc_ref)
        state_ref[...] = jnp.zeros_like(state_ref)

    def conv_silu(in_ref, tail_ref, w_ref, b_ref):
        cur = in_ref[...]
        shifted = _dot(shift_ref[...], cur)
        cur_f = cur.astype(F32)
        acc = b_ref[...] + w_ref[taps:taps + 1, :] * cur_f
        for k in range(taps):
            acc = acc + w_ref[k:k + 1, :] * shifted[k * chunk:(k + 1) * chunk]
        tail = tail_ref[...]
        row = lax.broadcasted_iota(I32, tail.shape, 0)
        fix = jnp.zeros_like(tail)
        for k in range(taps):
            back = taps - k
            fix = fix + jnp.where(row < back,
                                  w_ref[k:k + 1, :] * pltpu.roll(tail, back, 0), 0.0)
        acc = jnp.concatenate([acc[:SUBLANES] + fix, acc[SUBLANES:]], axis=0)
        tail_ref[...] = cur_f[chunk - SUBLANES:]
        return _silu(acc)

    xs = conv_silu(xs_ref, tx_ref, cwx_ref, cbx_ref)
    bm = conv_silu(bm_ref, tb_ref, cwb_ref, cbb_ref)
    cm = conv_silu(cm_ref, tc_ref, cwc_ref, cbc_ref)
    xs_b = xs.astype(BF16)

    dt = jax.nn.softplus(dt_ref[...] + dtb_ref[...])
    a2 = -jnp.exp(alog_ref[...]) * LOG2_E
    r_io = lax.broadcasted_iota(I32, (chunk, chunk), 0)
    c_io = lax.broadcasted_iota(I32, (chunk, chunk), 1)
    tril = r_io >= c_io
    a_cum = jnp.dot(tril.astype(F32), dt * a2, precision=lax.Precision.HIGHEST,
                    preferred_element_type=F32)
    total = a_cum[chunk - 1:chunk, :]
    to_end_t = (jnp.exp2(total - a_cum) * dt).T
    src_t = (a_cum - jnp.log2(dt)).T
    first = lax.broadcasted_iota(I32, (1, LANES), 1) < SSM_HEAD_DIM

    for g in range(N_SSM_GROUPS):
        bg = bm[:, g * D_STATE:(g + 1) * D_STATE]
        cg = cm[:, g * D_STATE:(g + 1) * D_STATE]
        cb = _dot_nt(cg.astype(BF16), bg.astype(BF16))
        bg_t = bg.T
        for pair in range(HEADS_PER_GROUP // 2):
            h0 = g * HEADS_PER_GROUP + 2 * pair
            lo = h0 * SSM_HEAD_DIM
            lhs, bgw, e_last = [], [], []
            for h in (h0, h0 + 1):
                col = jnp.broadcast_to(a_cum[:, h:h + 1], (chunk, chunk))
                decay = jnp.exp2(jnp.where(tril, col - src_t[h:h + 1, :], -jnp.inf))
                ecol = jnp.exp2(col)
                lhs.append(jnp.concatenate(
                    [(cb * decay).astype(BF16), (cg * ecol).astype(BF16)], axis=1))
                bgw.append((bg_t * to_end_t[h:h + 1, :]).astype(BF16))
                e_last.append(ecol[chunk - 1:chunk, :])
            xs_pair = xs_b[:, lo:lo + LANES]
            prev = state_ref[:, lo:lo + LANES]
            rhs = jnp.concatenate([xs_pair, prev.astype(BF16)], axis=0)
            r = _dot(jnp.concatenate(lhs, axis=0), rhs)
            yacc_ref[:, lo:lo + LANES] = jnp.where(first, r[:chunk], r[chunk:])
            nw = _dot(jnp.concatenate(bgw, axis=0), xs_pair)
            keep = jnp.where(first, e_last[0], e_last[1])
            state_ref[:, lo:lo + LANES] = prev * keep + jnp.where(
                first, nw[:D_STATE], nw[D_STATE:])

    y = yacc_ref[...] + xs * dskip_ref[...]
    zf = z_ref[...].astype(F32)
    y = y * _silu(zf)
    gw = D_INNER // N_SSM_GROUPS
    for g in range(N_SSM_GROUPS):
        yg = y[:, g * gw:(g + 1) * gw]
        yn = yg * _rms_scale(yg) * ng_ref[:, g * gw:(g + 1) * gw]
        y_ref[:, g * gw:(g + 1) * gw] = yn.astype(BF16)


def _ssd(proj_b, proj_f, conv_w, conv_b, dt_bias, a_log, d_skip, ssm_norm_g):
    b, s, _ = proj_b.shape
    chunk = SSD_CHUNK
    assert chunk == D_STATE == LANES and s % chunk == 0
    pad = LANES - N_SSM_HEADS
    taps = CONV_WIDTH - 1
    t_io = np.arange(chunk)
    shift = np.concatenate(
        [(t_io[:, None] - (taps - k) == t_io[None, :]) for k in range(taps)], axis=0)
    shift = jnp.asarray(shift, BF16)
    cw_x, cw_b, cw_c = (conv_w[:, :D_INNER], conv_w[:, D_INNER:D_INNER + BC_WIDTH],
                        conv_w[:, D_INNER + BC_WIDTH:])
    cb = conv_b.reshape(1, -1)
    cb_x, cb_b, cb_c = (cb[:, :D_INNER], cb[:, D_INNER:D_INNER + BC_WIDTH],
                        cb[:, D_INNER + BC_WIDTH:])
    dtb = jnp.pad(dt_bias, (0, pad)).reshape(1, LANES)
    alog = jnp.pad(a_log, (0, pad)).reshape(1, LANES)
    dskip = jnp.repeat(d_skip, SSM_HEAD_DIM).reshape(1, D_INNER)
    const = lambda bi, i: (0, 0)
    return pl.pallas_call(
        functools.partial(_ssd_kernel, chunk=chunk),
        grid=(b, s // chunk),
        in_specs=[
            pl.BlockSpec((None, chunk, D_INNER), lambda bi, i: (bi, i, PB_Z // D_INNER)),
            pl.BlockSpec((None, chunk, D_INNER), lambda bi, i: (bi, i, PB_XS // D_INNER)),
            pl.BlockSpec((None, chunk, BC_WIDTH), lambda bi, i: (bi, i, PB_BM // BC_WIDTH)),
            pl.BlockSpec((None, chunk, BC_WIDTH), lambda bi, i: (bi, i, PB_CM // BC_WIDTH)),
            pl.BlockSpec((None, chunk, LANES), lambda bi, i: (bi, i, PF_DT // LANES)),
            pl.BlockSpec((taps * chunk, chunk), const),
            pl.BlockSpec((CONV_WIDTH, D_INNER), const),
            pl.BlockSpec((CONV_WIDTH, BC_WIDTH), const),
            pl.BlockSpec((CONV_WIDTH, BC_WIDTH), const),
            pl.BlockSpec((1, D_INNER), const),
            pl.BlockSpec((1, BC_WIDTH), const),
            pl.BlockSpec((1, BC_WIDTH), const),
            pl.BlockSpec((1, LANES), const),
            pl.BlockSpec((1, LANES), const),
            pl.BlockSpec((1, D_INNER), const),
            pl.BlockSpec((1, D_INNER), const),
        ],
        out_specs=pl.BlockSpec((None, chunk, D_INNER), lambda bi, i: (bi, i, 0)),
        out_shape=jax.ShapeDtypeStruct((b, s, D_INNER), BF16),
        scratch_shapes=[
            pltpu.VMEM((SUBLANES, D_INNER), F32),
            pltpu.VMEM((SUBLANES, BC_WIDTH), F32),
            pltpu.VMEM((SUBLANES, BC_WIDTH), F32),
            pltpu.VMEM((D_STATE, D_INNER), F32),
            pltpu.VMEM((chunk, D_INNER), F32),
        ],
        compiler_params=_params("arbitrary", "arbitrary"),
        name="ssd_mixer",
    )(proj_b, proj_b, proj_b, proj_b, proj_f, shift, cw_x, cw_b, cw_c, cb_x, cb_b, cb_c,
      dtb, alog, dskip, ssm_norm_g.reshape(1, D_INNER))


def _mix_out_kernel(att_ref, y_ref, ga_ref, gs_ref, x_ref, gate_ref, g_ref,
                    woa_ref, wos_ref, wout_ref, o_ref):
    ba = _dot(att_ref[...], woa_ref[...])
    bs = _dot(y_ref[...], wos_ref[...])
    merged = (jax.nn.sigmoid(ga_ref[...].astype(F32)) * ba
              + jax.nn.sigmoid(gs_ref[...].astype(F32)) * bs)
    out = _dot(merged.astype(BF16), wout_ref[...])
    normed = out * _rms_scale(out) * g_ref[...]
    o_ref[...] = x_ref[...] + gate_ref[...] * normed


def _mix_out(att, y, proj_b, x, gate, g, woa, wos, wout):
    b, s, d = x.shape
    tm = min(OUT_TM, s)
    const = lambda bi, i: (0, 0)
    return pl.pallas_call(
        _mix_out_kernel,
        grid=(b, s // tm),
        in_specs=[
            pl.BlockSpec((None, tm, ATT_WIDTH), lambda bi, i: (bi, i, 0)),
            pl.BlockSpec((None, tm, D_INNER), lambda bi, i: (bi, i, 0)),
            pl.BlockSpec((None, tm, d), lambda bi, i: (bi, i, PB_GATT // d)),
            pl.BlockSpec((None, tm, d), lambda bi, i: (bi, i, PB_GSSD // d)),
            pl.BlockSpec((None, tm, d), lambda bi, i: (bi, i, 0)),
            pl.BlockSpec((None, 1, d), lambda bi, i: (bi, 0, 0)),
            pl.BlockSpec((1, d), const),
            pl.BlockSpec((ATT_WIDTH, d), const),
            pl.BlockSpec((D_INNER, d), const),
            pl.BlockSpec((d, d), const),
        ],
        out_specs=pl.BlockSpec((None, tm, d), lambda bi, i: (bi, i, 0)),
        out_shape=jax.ShapeDtypeStruct((b, s, d), F32),
        compiler_params=_params("arbitrary", "arbitrary"),
        name="mix_out",
    )(att, y, proj_b, proj_b, x, gate, g, woa, wos, wout)


def _mlp_kernel(x_ref, g_in_ref, sc_ref, sh_ref, wup_ref, wdn_ref, gate_ref,
                g_out_ref, o_ref, h_ref, acc_ref):
    j = pl.program_id(2)

    @pl.when(j == 0)
    def _():
        x = x_ref[...]
        h = x * _rms_scale(x) * g_in_ref[...]
        h = h * (1.0 + sc_ref[...]) + sh_ref[...]
        h_ref[...] = h.astype(BF16)
        acc_ref[...] = jnp.zeros_like(acc_ref)

    up = jnp.maximum(_dot(h_ref[...], wup_ref[...]), 0.0)
    acc_ref[...] += _dot((up * up).astype(BF16), wdn_ref[...])

    @pl.when(j == pl.num_programs(2) - 1)
    def _():
        y = acc_ref[...]
        normed = y * _rms_scale(y) * g_out_ref[...]
        o_ref[...] = x_ref[...] + gate_ref[...] * normed


def _mlp(x, g_in, sc, sh, wup, wdn, gate, g_out):
    b, s, d = x.shape
    ff = wup.shape[1]
    tm = min(MLP_TM, s)
    per_b = lambda bi, i, j: (bi, 0, 0)
    const = lambda bi, i, j: (0, 0)
    return pl.pallas_call(
        _mlp_kernel,
        grid=(b, s // tm, ff // MLP_TF),
        in_specs=[
            pl.BlockSpec((None, tm, d), lambda bi, i, j: (bi, i, 0)),
            pl.BlockSpec((1, d), const),
            pl.BlockSpec((None, 1, d), per_b),
            pl.BlockSpec((None, 1, d), per_b),
            pl.BlockSpec((d, MLP_TF), lambda bi, i, j: (0, j)),
            pl.BlockSpec((MLP_TF, d), lambda bi, i, j: (j, 0)),
            pl.BlockSpec((None, 1, d), per_b),
            pl.BlockSpec((1, d), const),
        ],
        out_specs=pl.BlockSpec((None, tm, d), lambda bi, i, j: (bi, i, 0)),
        out_shape=jax.ShapeDtypeStruct((b, s, d), F32),
        scratch_shapes=[pltpu.VMEM((tm, d), BF16), pltpu.VMEM((tm, d), F32)],
        compiler_params=_params("arbitrary", "arbitrary", "arbitrary"),
        name="mlp",
    )(x, g_in, sc, sh, wup, wdn, gate, g_out)


def _split_offsets():
    sizes = (ATT_WIDTH, KV_RANK, IDX_HEADS * IDX_DIM, IDX_DIM, IDX_HEADS,
             D_INNER, D_INNER, BC_WIDTH, BC_WIDTH, N_SSM_HEADS, ATT_WIDTH, ATT_WIDTH)
    names = ("q", "kv", "qidx", "kidx", "widx", "z", "xs", "bm", "cm", "dt",
             "gatt", "gssd")
    offs = np.concatenate([[0], np.cumsum(sizes)])
    return {n: (int(offs[i]), int(offs[i + 1])) for i, n in enumerate(names)}


def _pack_w_in(w_in):
    sl = _split_offsets()
    col = lambda n: w_in[..., sl[n][0]:sl[n][1]]
    wb = jnp.concatenate([col(n) for n in
                          ("z", "xs", "q", "gatt", "gssd", "bm", "cm", "qidx")],
                         axis=-1).astype(BF16)
    lead = w_in.shape[:-1]
    zeros = lambda n: jnp.zeros(lead + (n,), w_in.dtype)
    wf = jnp.concatenate([
        col("kv"), col("kidx"), col("widx"), zeros(LANES - IDX_DIM - IDX_HEADS),
        col("dt"), zeros(LANES - N_SSM_HEADS)], axis=-1).astype(BF16)
    return wb, wf


def kernel(x, c, ada_w, ada_b, norm_g, w_in, kv_norm_g, kidx_norm_g, w_uk, w_uv,
           conv_w, conv_b, dt_bias, a_log, d_skip, ssm_norm_g, w_o_att, w_o_ssd,
           w_out, w_up, w_down):
    depth = ada_w.shape[0]
    d = x.shape[-1]
    mod = _modulation(c, ada_w, ada_b)
    wb_all, wf_all = _pack_w_in(w_in)
    wuk_all = jnp.transpose(w_uk, (0, 2, 1, 3)).astype(BF16)
    wuvt_all = jnp.transpose(w_uv, (0, 2, 3, 1)).astype(BF16)
    woa_all = w_o_att.astype(BF16)
    wos_all = w_o_ssd.astype(BF16)
    wout_all = w_out.astype(BF16)
    wup_all = w_up.astype(BF16)
    wdn_all = w_down.astype(BF16)

    for layer in range(depth):
        sh_m, sc_m, gt_m, sh_f, sc_f, gt_f = (mod[layer, :, k] for k in range(N_MOD))
        ng = lambda k: norm_g[layer, k].reshape(1, d)
        proj_b, proj_f = _input_projection(x, ng(0), sc_m, sh_m,
                                           wb_all[layer], wf_all[layer])
        ckv, ckvt, kidx, wt = _latent_prep(
            proj_f, kv_norm_g[layer].reshape(1, KV_RANK),
            kidx_norm_g[layer].reshape(1, IDX_DIM))
        att = _dsa_attention(proj_b, wt, ckv, ckvt, kidx,
                             wuk_all[layer], wuvt_all[layer])
        y = _ssd(proj_b, proj_f, conv_w[layer], conv_b[layer], dt_bias[layer],
                 a_log[layer], d_skip[layer], ssm_norm_g[layer])
        x = _mix_out(att, y, proj_b, x, gt_m, ng(1), woa_all[layer],
                     wos_all[layer], wout_all[layer])
        x = _mlp(x, ng(2), sc_f, sh_f, wup_all[layer], wdn_all[layer], gt_f, ng(3))
    return x
```

```python
import functools

import jax
import jax.numpy as jnp
import numpy as np
from jax import lax
from jax.experimental import pallas as pl
from jax.experimental.pallas import tpu as pltpu

F32 = jnp.float32
BF16 = jnp.bfloat16
I32 = jnp.int32
I16 = jnp.int16

N_ATT_HEADS = 8
ATT_HEAD_DIM = 128
ATT_WIDTH = N_ATT_HEADS * ATT_HEAD_DIM
KV_RANK = 256
IDX_HEADS = 8
IDX_DIM = 64
TOPK_MAX = 256
D_STATE = 128
SSM_HEAD_DIM = 64
N_SSM_GROUPS = 4
HEADS_PER_GROUP = 8
N_SSM_HEADS = N_SSM_GROUPS * HEADS_PER_GROUP
D_INNER = N_SSM_HEADS * SSM_HEAD_DIM
BC_WIDTH = N_SSM_GROUPS * D_STATE
CONV_WIDTH = 4
N_MOD = 6
EPS = 1e-6

LANES = 128
SUBLANES = 8
VMEM_LIMIT_BYTES = 52 * 1024 * 1024

MXU_WIDTH = 256
Q_TILE = 256
KEY_CHUNK = 256
PACK_ROWS = 16
CKVT_ROWS = KV_RANK + PACK_ROWS
SSD_CHUNK = 128
INPROJ_TM = 2048
INPROJ_TN = 512
OUT_TM = 512
MLP_TM = 1024
MLP_TF = 1024
MOD_TN = 1536

INT_MIN = -2147483648
I16_MIN = -32768
I16_MAX = 32767
HALF_BIAS = 32768
LOG2_E = 1.4426950408889634
COUNT_ACCS = 8
SELECT_GROUP = 5
NEG_BIG = -1e30

PB_Z, PB_XS, PB_BM, PB_CM, PB_Q, PB_GATT, PB_GSSD, PB_QIDX = (
    0, 2048, 4096, 4608, 5120, 6144, 7168, 8192)
PB_WIDTH = 8704
PF_KV, PF_KIDX, PF_DT = 0, 256, 384
PF_WIDTH = 512


def _params(*sem):
    return pltpu.CompilerParams(dimension_semantics=sem,
                                vmem_limit_bytes=VMEM_LIMIT_BYTES)


def _dot(a, b):
    return jnp.dot(a, b, preferred_element_type=F32)


def _dot_nt(a, b):
    return lax.dot_general(a, b, (((1,), (1,)), ((), ())),
                           preferred_element_type=F32)


def _silu(x):
    return x * jax.nn.sigmoid(x)


def _rms_scale(x):
    return lax.rsqrt(jnp.mean(x * x, axis=-1, keepdims=True) + EPS)


def _mod_kernel(c_ref, w_ref, b_ref, o_ref):
    c = c_ref[...]
    c_act = (c * jax.nn.sigmoid(c)).astype(BF16)
    o_ref[...] = _dot(c_act, w_ref[...].astype(BF16)) + b_ref[...]


def _modulation(c, ada_w, ada_b):
    n_layers, d, n = ada_w.shape
    b = c.shape[0]
    rows = ((b + SUBLANES - 1) // SUBLANES) * SUBLANES
    c_pad = jnp.zeros((rows, d), F32).at[:b].set(c)
    out = pl.pallas_call(
        _mod_kernel,
        grid=(n_layers, n // MOD_TN),
        in_specs=[
            pl.BlockSpec((rows, d), lambda l, j: (0, 0)),
            pl.BlockSpec((None, d, MOD_TN), lambda l, j: (l, 0, j)),
            pl.BlockSpec((None, 1, MOD_TN), lambda l, j: (l, 0, j)),
        ],
        out_specs=pl.BlockSpec((None, rows, MOD_TN), lambda l, j: (l, 0, j)),
        out_shape=jax.ShapeDtypeStruct((n_layers, rows, n), F32),
        compiler_params=_params("arbitrary", "arbitrary"),
        name="adaln_mod",
    )(c_pad, ada_w, ada_b.reshape(n_layers, 1, n))
    return out[:, :b].reshape(n_layers, b, N_MOD, 1, d)


def _inproj_kernel(x_ref, g_ref, sc_ref, sh_ref, wb_ref, wf_ref,
                   ob_ref, of_ref, h_ref, *, n_bf16_tiles):
    j = pl.program_id(2)

    @pl.when(j == 0)
    def _():
        x = x_ref[...]
        h = x * _rms_scale(x) * g_ref[...]
        h = h * (1.0 + sc_ref[...]) + sh_ref[...]
        h_ref[...] = h.astype(BF16)

    @pl.when(j < n_bf16_tiles)
    def _():
        ob_ref[...] = _dot(h_ref[...], wb_ref[...]).astype(BF16)

    @pl.when(j == n_bf16_tiles)
    def _():
        of_ref[...] = _dot(h_ref[...], wf_ref[...])


def _input_projection(x, g, sc, sh, wb, wf):
    b, s, d = x.shape
    tm = min(INPROJ_TM, s)
    nj = PB_WIDTH // INPROJ_TN
    last = nj - 1
    return pl.pallas_call(
        functools.partial(_inproj_kernel, n_bf16_tiles=nj),
        grid=(b, s // tm, nj + 1),
        in_specs=[
            pl.BlockSpec((None, tm, d), lambda bi, i, j: (bi, i, 0)),
            pl.BlockSpec((1, d), lambda bi, i, j: (0, 0)),
            pl.BlockSpec((None, 1, d), lambda bi, i, j: (bi, 0, 0)),
            pl.BlockSpec((None, 1, d), lambda bi, i, j: (bi, 0, 0)),
            pl.BlockSpec((d, INPROJ_TN),
                         lambda bi, i, j: (0, jnp.minimum(j, last))),
            pl.BlockSpec((d, PF_WIDTH), lambda bi, i, j: (0, 0)),
        ],
        out_specs=[
            pl.BlockSpec((None, tm, INPROJ_TN),
                         lambda bi, i, j: (bi, i, jnp.minimum(j, last))),
            pl.BlockSpec((None, tm, PF_WIDTH), lambda bi, i, j: (bi, i, 0)),
        ],
        out_shape=[
            jax.ShapeDtypeStruct((b, s, PB_WIDTH), BF16),
            jax.ShapeDtypeStruct((b, s, PF_WIDTH), F32),
        ],
        scratch_shapes=[pltpu.VMEM((tm, d), BF16)],
        compiler_params=_params("arbitrary", "arbitrary", "arbitrary"),
        name="norm_inproj",
    )(x, g, sc, sh, wb, wf)


def _prep_kernel(kv_ref, kw_ref, gkv_ref, gk_ref, ckv_ref, ckvt_ref,
                 kidx_ref, wt_ref, *, n_chunks):
    kv = kv_ref[...]
    ckv = kv * _rms_scale(kv) * gkv_ref[...]
    ckv_ref[...] = ckv.astype(BF16)
    extra_row = lax.broadcasted_iota(I32, (CKVT_ROWS - KV_RANK, KEY_CHUNK), 0)
    extra = jnp.where(extra_row == 0, 1.0, 0.0).astype(BF16)
    for c in range(n_chunks):
        blk = ckv[c * KEY_CHUNK:(c + 1) * KEY_CHUNK, :]
        ckvt_ref[c] = jnp.concatenate([blk.T.astype(BF16), extra], axis=0)
    kw = kw_ref[...]
    k = kw[:, :IDX_DIM]
    kn = k * _rms_scale(k) * gk_ref[...]
    kidx_ref[...] = kn.astype(BF16)
    kwt = kw.T
    wt_ref[...] = kwt[IDX_DIM:IDX_DIM + IDX_HEADS, :] * (IDX_HEADS ** -0.5)


def _latent_prep(proj_f, kv_norm_g, kidx_norm_g):
    b, s, _ = proj_f.shape
    ts = min(1024, s)
    n_chunks = ts // KEY_CHUNK
    return pl.pallas_call(
        functools.partial(_prep_kernel, n_chunks=n_chunks),
        grid=(b, s // ts),
        in_specs=[
            pl.BlockSpec((None, ts, KV_RANK),
                         lambda bi, i: (bi, i, PF_KV // KV_RANK)),
            pl.BlockSpec((None, ts, LANES),
                         lambda bi, i: (bi, i, PF_KIDX // LANES)),
            pl.BlockSpec((1, KV_RANK), lambda bi, i: (0, 0)),
            pl.BlockSpec((1, IDX_DIM), lambda bi, i: (0, 0)),
        ],
        out_specs=[
            pl.BlockSpec((None, ts, KV_RANK), lambda bi, i: (bi, i, 0)),
            pl.BlockSpec((None, n_chunks, CKVT_ROWS, KEY_CHUNK),
                         lambda bi, i: (bi, i, 0, 0)),
            pl.BlockSpec((None, ts, IDX_DIM), lambda bi, i: (bi, i, 0)),
            pl.BlockSpec((None, IDX_HEADS, ts), lambda bi, i: (bi, 0, i)),
        ],
        out_shape=[
            jax.ShapeDtypeStruct((b, s, KV_RANK), BF16),
            jax.ShapeDtypeStruct((b, s // KEY_CHUNK, CKVT_ROWS, KEY_CHUNK), BF16),
            jax.ShapeDtypeStruct((b, s, IDX_DIM), BF16),
            jax.ShapeDtypeStruct((b, IDX_HEADS, s), F32),
        ],
        compiler_params=_params("arbitrary", "arbitrary"),
        name="latent_prep",
    )(proj_f, proj_f, kv_norm_g, kidx_norm_g)


def _dsa_kernel(q_ref, qidx_ref, wt_ref, ckv_ref, ckvt_ref, kidx_ref,
                wuk_ref, wuvt_ref, o_ref,
                keys_ref, khi_ref, klo_ref, lom_ref, qall_ref, qabs_ref, acc_ref,
                tsel_ref, p_ref,
                *, topk, pos_bits):
    qb = pl.program_id(1)
    n_chunks = (qb * Q_TILE + Q_TILE + KEY_CHUNK - 1) // KEY_CHUNK
    q_pos = qb * Q_TILE + lax.broadcasted_iota(I32, (KEY_CHUNK, Q_TILE), 1)
    row_iota = lax.broadcasted_iota(I32, (KEY_CHUNK, Q_TILE), 0)

    for h in range(IDX_HEADS):
        qall_ref[h * Q_TILE:(h + 1) * Q_TILE, :] = (
            qidx_ref[:, h * IDX_DIM:(h + 1) * IDX_DIM])

    n_full = n_chunks // 2
    n_pairs = (n_chunks + 1) // 2
    pair_rows = 2 * KEY_CHUNK

    def score_chunk(c):
        start = pl.multiple_of(c * KEY_CHUNK, KEY_CHUNK)
        ks = kidx_ref[pl.ds(start, KEY_CHUNK), :]
        logits = _dot_nt(ks, qall_ref[...])
        score = jnp.zeros((KEY_CHUNK, Q_TILE), F32)
        for h in range(IDX_HEADS):
            lg = logits[:, h * Q_TILE:(h + 1) * Q_TILE]
            score = score + wt_ref[h:h + 1, :] * jnp.maximum(lg, 0.0)
        bits = pltpu.bitcast(score, I32)
        key = bits ^ ((bits >> 31) & 0x7FFFFFFF)
        key = jnp.where(start + row_iota > q_pos, INT_MIN, key)
        keys_ref[pl.ds(start, KEY_CHUNK), :] = key
        khi_ref[pl.ds(start, KEY_CHUNK), :] = (key >> 16).astype(I16)
        klo_ref[pl.ds(start, KEY_CHUNK), :] = ((key & 0xFFFF) - HALF_BIAS).astype(I16)

    def score_pair(i, carry):
        score_chunk(2 * i)
        score_chunk(2 * i + 1)
        return carry

    lax.fori_loop(0, n_full, score_pair, 0)

    @pl.when(n_chunks % 2 == 1)
    def _():
        score_chunk(n_chunks - 1)
        pad = pl.multiple_of(n_chunks * KEY_CHUNK, KEY_CHUNK)
        never = jnp.full((KEY_CHUNK, Q_TILE), I16_MIN, I16)
        khi_ref[pl.ds(pad, KEY_CHUNK), :] = never
        klo_ref[pl.ds(pad, KEY_CHUNK), :] = never

    zero = jnp.zeros((1, Q_TILE), I32)

    def count16(ref, t):
        t16 = jnp.broadcast_to(t, (PACK_ROWS, Q_TILE)).astype(I16)

        def body(c, accs):
            start = pl.multiple_of(c * pair_rows, pair_rows)
            blk = ref[pl.ds(start, pair_rows), :]
            accs = list(accs)
            for r in range(pair_rows // PACK_ROWS):
                hit = jnp.where(blk[r * PACK_ROWS:(r + 1) * PACK_ROWS] >= t16,
                                jnp.int16(1), jnp.int16(0))
                accs[r % COUNT_ACCS] = accs[r % COUNT_ACCS] + hit
            return tuple(accs)
        zeros16 = jnp.zeros((PACK_ROWS, Q_TILE), I16)
        accs = lax.fori_loop(0, n_pairs, body, (zeros16,) * COUNT_ACCS)
        total = accs[0].astype(I32)
        for k in range(1, COUNT_ACCS):
            total = total + accs[k].astype(I32)
        return jnp.sum(total, axis=0, keepdims=True)

    def refine16(ref, target, cand, cand_cnt, n_bits, live=None):
        def bit_step(i, carry):
            cand, cand_cnt = carry
            trial = cand | (jnp.int32(1) << (n_bits - 1 - i))
            cnt = count16(ref, trial)
            keep = cnt >= target
            return jnp.where(keep, trial, cand), jnp.where(keep, cnt, cand_cnt)

        if live is None:
            return lax.fori_loop(0, n_bits, bit_step, (cand, cand_cnt))

        def unsettled(cnt):
            return jnp.max(jnp.where(live & (cnt != target), 1, 0))

        def group_step(state):
            g, cand, cand_cnt, _ = state
            for j in range(SELECT_GROUP):
                cand, cand_cnt = bit_step(g * SELECT_GROUP + j, (cand, cand_cnt))
            return g + 1, cand, cand_cnt, unsettled(cand_cnt)

        _, cand, cand_cnt, _ = lax.while_loop(
            lambda st: (st[0] < n_bits // SELECT_GROUP) & (st[3] > 0),
            group_step, (jnp.int32(0), cand, cand_cnt, unsettled(cand_cnt)))
        return cand, cand_cnt

    def select16(ref, target, floor_cnt, live=None):
        cnt0 = count16(ref, zero)
        ok = cnt0 >= target
        cand = jnp.where(ok, zero, jnp.full((1, Q_TILE), I16_MIN, I32))
        return refine16(ref, target, cand, jnp.where(ok, cnt0, floor_cnt), 15, live)

    thr_hi, cnt_ge_hi = select16(khi_ref, topk, zero)
    cnt_above = count16(khi_ref, thr_hi + 1)
    real = thr_hi != I16_MIN

    thr_hi16 = jnp.broadcast_to(thr_hi, (PACK_ROWS, Q_TILE)).astype(I16)

    def mask_low(c, carry):
        start = pl.multiple_of(c * pair_rows, pair_rows)
        for r in range(pair_rows // PACK_ROWS):
            rows = pl.ds(start + r * PACK_ROWS, PACK_ROWS)
            lom_ref[rows, :] = jnp.where(khi_ref[rows, :] == thr_hi16,
                                         klo_ref[rows, :], jnp.int16(I16_MIN))
        return carry

    lax.fori_loop(0, n_pairs, mask_low, 0)
    thr_lo, cnt_lo = select16(lom_ref, topk - cnt_above, cnt_ge_hi - cnt_above,
                              live=real)

    thr = jnp.where(real, (thr_hi << 16) | (thr_lo + HALF_BIAS), INT_MIN)
    thr_cnt = cnt_above + cnt_lo

    tsel_ref[...] = jnp.where(real, jnp.int32(2 ** 30), jnp.int32(-1))
    n_tied = jnp.max(jnp.where(real & (thr_cnt > topk), 1, 0))

    @pl.when(n_tied > 0)
    def _():
        cnt_gt = cnt_above + jnp.where(
            thr_lo < I16_MAX, count16(lom_ref, thr_lo + 1), 0)
        need = topk - cnt_gt
        thr_lo16 = jnp.broadcast_to(thr_lo, (PACK_ROWS, Q_TILE)).astype(I16)
        pack_iota = lax.broadcasted_iota(I32, (PACK_ROWS, Q_TILE), 0)

        def reversed_positions(c, carry):
            start = pl.multiple_of(c * pair_rows, pair_rows)
            for r in range(pair_rows // PACK_ROWS):
                rows = pl.ds(start + r * PACK_ROWS, PACK_ROWS)
                tied = ((khi_ref[rows, :] == thr_hi16)
                        & (lom_ref[rows, :] == thr_lo16))
                rev = (I16_MAX - (start + r * PACK_ROWS) - pack_iota).astype(I16)
                klo_ref[rows, :] = jnp.where(tied, rev, jnp.int16(I16_MIN))
            return carry

        lax.fori_loop(0, n_pairs, reversed_positions, 0)
        base = jnp.full((1, Q_TILE), I16_MAX + 1 - 2 ** pos_bits, I32)
        last_rev, _ = refine16(klo_ref, need, base, zero, pos_bits)
        tsel_ref[...] = jnp.where(real, I16_MAX - last_rev, jnp.int32(-1))

    tsel = tsel_ref[...]

    scale = ATT_HEAD_DIM ** -0.5 * LOG2_E
    for h in range(N_ATT_HEADS):
        qh = q_ref[:, h * ATT_HEAD_DIM:(h + 1) * ATT_HEAD_DIM]
        qa = _dot_nt(wuk_ref[h], qh) * scale
        qabs_ref[:, h * Q_TILE:(h + 1) * Q_TILE] = qa.astype(BF16)

    width = N_ATT_HEADS * Q_TILE
    acc_ref[...] = jnp.zeros_like(acc_ref)

    def probabilities(c, slot, m):
        start = pl.multiple_of(c * KEY_CHUNK, KEY_CHUNK)
        kv = ckv_ref[pl.ds(start, KEY_CHUNK), :]
        kk = keys_ref[pl.ds(start, KEY_CHUNK), :]
        pos = start + row_iota
        bias = jnp.where(
            kk > thr, 0.0,
            jnp.where(kk == thr, jnp.where(pos <= tsel, 0.0, NEG_BIG), NEG_BIG))
        m_out, alphas = [], []
        for grp in range(width // MXU_WIDTH):
            lo = grp * MXU_WIDTH
            st = _dot(kv, qabs_ref[:, lo:lo + MXU_WIDTH])
            for hh in range(MXU_WIDTH // Q_TILE):
                a, b = lo + hh * Q_TILE, lo + (hh + 1) * Q_TILE
                s_h = st[:, hh * Q_TILE:(hh + 1) * Q_TILE] + bias
                m_old = m[:, a:b]
                m_new = jnp.maximum(m_old, jnp.max(s_h, axis=0, keepdims=True))
                alpha = jnp.exp2(m_old - m_new)
                p = jnp.exp2(s_h - m_new)
                m_out.append(m_new)
                alphas.append(alpha)
                p_ref[slot, :, a:b] = p.astype(BF16)
        cat = lambda xs: jnp.concatenate(xs, axis=1)
        return cat(m_out), cat(alphas)

    def accumulate(c, slot, alpha):
        kvt = ckvt_ref[c]
        for grp in range(width // MXU_WIDTH):
            lo, hi = grp * MXU_WIDTH, (grp + 1) * MXU_WIDTH
            pv = _dot(kvt, p_ref[slot, :, lo:hi])
            acc_ref[:, lo:hi] = acc_ref[:, lo:hi] * alpha[:, lo:hi] + pv

    p_ref[1] = jnp.zeros((KEY_CHUNK, width), BF16)

    def attn_pair(i, carry):
        m, alpha_prev = carry
        c0 = 2 * i
        m, alpha0 = probabilities(c0, 0, m)
        accumulate(jnp.maximum(c0 - 1, 0), 1, alpha_prev)
        m, alpha1 = probabilities(c0 + 1, 1, m)
        accumulate(c0, 0, alpha0)
        return m, alpha1

    m0 = jnp.full((1, width), NEG_BIG, F32)
    ones = jnp.ones((1, width), F32)
    m_last, alpha_last = lax.fori_loop(0, n_full, attn_pair, (m0, ones))

    @pl.when(n_chunks % 2 == 0)
    def _():
        accumulate(n_chunks - 1, 1, alpha_last)

    @pl.when(n_chunks % 2 == 1)
    def _():
        last = n_chunks - 1
        _, alpha0 = probabilities(last, 0, m_last)
        accumulate(jnp.maximum(last - 1, 0), 1, alpha_last)
        accumulate(last, 0, alpha0)

    denom = acc_ref[KV_RANK:KV_RANK + 1, :]
    o_lat = (acc_ref[0:KV_RANK, :] * (1.0 / denom)).astype(BF16)
    for h in range(N_ATT_HEADS):
        ot = _dot(wuvt_ref[h], o_lat[:, h * Q_TILE:(h + 1) * Q_TILE])
        o_ref[:, h * ATT_HEAD_DIM:(h + 1) * ATT_HEAD_DIM] = ot.T.astype(BF16)


def _dsa_attention(proj_b, wt, ckv, ckvt, kidx, wuk, wuvt):
    b, s, _ = proj_b.shape
    topk = min(TOPK_MAX, s // 4)
    assert s % (2 * KEY_CHUNK) == 0 and s <= I16_MAX + 1
    width = N_ATT_HEADS * Q_TILE
    return pl.pallas_call(
        functools.partial(_dsa_kernel, topk=topk, pos_bits=(s - 1).bit_length()),
        grid=(b, s // Q_TILE),
        in_specs=[
            pl.BlockSpec((None, Q_TILE, ATT_WIDTH),
                         lambda bi, i: (bi, i, PB_Q // ATT_WIDTH)),
            pl.BlockSpec((None, Q_TILE, IDX_HEADS * IDX_DIM),
                         lambda bi, i: (bi, i, PB_QIDX // (IDX_HEADS * IDX_DIM))),
            pl.BlockSpec((None, IDX_HEADS, Q_TILE), lambda bi, i: (bi, 0, i)),
            pl.BlockSpec((None, s, KV_RANK), lambda bi, i: (bi, 0, 0)),
            pl.BlockSpec((None, s // KEY_CHUNK, CKVT_ROWS, KEY_CHUNK),
                         lambda bi, i: (bi, 0, 0, 0)),
            pl.BlockSpec((None, s, IDX_DIM), lambda bi, i: (bi, 0, 0)),
            pl.BlockSpec((N_ATT_HEADS, KV_RANK, ATT_HEAD_DIM),
                         lambda bi, i: (0, 0, 0)),
            pl.BlockSpec((N_ATT_HEADS, ATT_HEAD_DIM, KV_RANK),
                         lambda bi, i: (0, 0, 0)),
        ],
        out_specs=pl.BlockSpec((None, Q_TILE, ATT_WIDTH), lambda bi, i: (bi, i, 0)),
        out_shape=jax.ShapeDtypeStruct((b, s, ATT_WIDTH), BF16),
        scratch_shapes=[
            pltpu.VMEM((s, Q_TILE), I32),
            pltpu.VMEM((s, Q_TILE), I16),
            pltpu.VMEM((s, Q_TILE), I16),
            pltpu.VMEM((s, Q_TILE), I16),
            pltpu.VMEM((IDX_HEADS * Q_TILE, IDX_DIM), BF16),
            pltpu.VMEM((KV_RANK, width), BF16),
            pltpu.VMEM((CKVT_ROWS, width), F32),
            pltpu.VMEM((1, Q_TILE), I32),
            pltpu.VMEM((2, KEY_CHUNK, width), BF16),
        ],
        compiler_params=_params("arbitrary", "arbitrary"),
        name="dsa_attention",
    )(proj_b, proj_b, wt, ckv, ckvt, kidx, wuk, wuvt)


def _ssd_kernel(z_ref, xs_ref, bm_ref, cm_ref, dt_ref, shift_ref,
                cwx_ref, cwb_ref, cwc_ref, cbx_ref, cbb_ref, cbc_ref,
                dtb_ref, alog_ref, dskip_ref, ng_ref, y_ref,
                tx_ref, tb_ref, tc_ref, state_ref, yacc_ref, *, chunk):
    ci = pl.program_id(1)
    taps = CONV_WIDTH - 1

    @pl.when(ci == 0)
    def _():
        tx_ref[...] = jnp.zeros_like(tx_ref)
        tb_ref[...] = jnp.zeros_like(tb_ref)
        tc_ref[...] = jnp.zeros_like(tc_ref)
        state_ref[...] = jnp.zeros_like(state_ref)

    def conv_silu(in_ref, tail_ref, w_ref, b_ref):
        cur = in_ref[...]
        shifted = _dot(shift_ref[...], cur)
        cur_f = cur.astype(F32)
        acc = b_ref[...] + w_ref[taps:taps + 1, :] * cur_f
        for k in range(taps):
            acc = acc + w_ref[k:k + 1, :] * shifted[k * chunk:(k + 1) * chunk]
        tail = tail_ref[...]
        row = lax.broadcasted_iota(I32, tail.shape, 0)
        fix = jnp.zeros_like(tail)
        for k in range(taps):
            back = taps - k
            fix = fix + jnp.where(row < back,
                                  w_ref[k:k + 1, :] * pltpu.roll(tail, back, 0), 0.0)
        acc = jnp.concatenate([acc[:SUBLANES] + fix, acc[SUBLANES:]], axis=0)
        tail_ref[...] = cur_f[chunk - SUBLANES:]
        return _silu(acc)

    xs = conv_silu(xs_ref, tx_ref, cwx_ref, cbx_ref)
    bm = conv_silu(bm_ref, tb_ref, cwb_ref, cbb_ref)
    cm = conv_silu(cm_ref, tc_ref, cwc_ref, cbc_ref)
    xs_b = xs.astype(BF16)

    dt = jax.nn.softplus(dt_ref[...] + dtb_ref[...])
    a2 = -jnp.exp(alog_ref[...]) * LOG2_E
    r_io = lax.broadcasted_iota(I32, (chunk, chunk), 0)
    c_io = lax.broadcasted_iota(I32, (chunk, chunk), 1)
    tril = r_io >= c_io
    a_cum = jnp.dot(tril.astype(F32), dt * a2, precision=lax.Precision.HIGHEST,
                    preferred_element_type=F32)
    total = a_cum[chunk - 1:chunk, :]
    to_end_t = (jnp.exp2(total - a_cum) * dt).T
    src_t = (a_cum - jnp.log2(dt)).T
    first = lax.broadcasted_iota(I32, (1, LANES), 1) < SSM_HEAD_DIM

    for g in range(N_SSM_GROUPS):
        bg = bm[:, g * D_STATE:(g + 1) * D_STATE]
        cg = cm[:, g * D_STATE:(g + 1) * D_STATE]
        cb = _dot_nt(cg.astype(BF16), bg.astype(BF16))
        bg_t = bg.T
        for pair in range(HEADS_PER_GROUP // 2):
            h0 = g * HEADS_PER_GROUP + 2 * pair
            lo = h0 * SSM_HEAD_DIM
            lhs, bgw, e_last = [], [], []
            for h in (h0, h0 + 1):
                col = jnp.broadcast_to(a_cum[:, h:h + 1], (chunk, chunk))
                decay = jnp.exp2(jnp.where(tril, col - src_t[h:h + 1, :], -jnp.inf))
                ecol = jnp.exp2(col)
                lhs.append(jnp.concatenate(
                    [(cb * decay).astype(BF16), (cg * ecol).astype(BF16)], axis=1))
                bgw.append((bg_t * to_end_t[h:h + 1, :]).astype(BF16))
                e_last.append(ecol[chunk - 1:chunk, :])
            xs_pair = xs_b[:, lo:lo + LANES]
            prev = state_ref[:, lo:lo + LANES]
            rhs = jnp.concatenate([xs_pair, prev.astype(BF16)], axis=0)
            r = _dot(jnp.concatenate(lhs, axis=0), rhs)
            yacc_ref[:, lo:lo + LANES] = jnp.where(first, r[:chunk], r[chunk:])
            nw = _dot(jnp.concatenate(bgw, axis=0), xs_pair)
            keep = jnp.where(first, e_last[0], e_last[1])
            state_ref[:, lo:lo + LANES] = prev * keep + jnp.where(
                first, nw[:D_STATE], nw[D_STATE:])

    y = yacc_ref[...] + xs * dskip_ref[...]
    zf = z_ref[...].astype(F32)
    y = y * _silu(zf)
    gw = D_INNER // N_SSM_GROUPS
    for g in range(N_SSM_GROUPS):
        yg = y[:, g * gw:(g + 1) * gw]
        yn = yg * _rms_scale(yg) * ng_ref[:, g * gw:(g + 1) * gw]
        y_ref[:, g * gw:(g + 1) * gw] = yn.astype(BF16)


def _ssd(proj_b, proj_f, conv_w, conv_b, dt_bias, a_log, d_skip, ssm_norm_g):
    b, s, _ = proj_b.shape
    chunk = SSD_CHUNK
    assert chunk == D_STATE == LANES and s % chunk == 0
    pad = LANES - N_SSM_HEADS
    taps = CONV_WIDTH - 1
    t_io = np.arange(chunk)
    shift = np.concatenate(
        [(t_io[:, None] - (taps - k) == t_io[None, :]) for k in range(taps)], axis=0)
    shift = jnp.asarray(shift, BF16)
    cw_x, cw_b, cw_c = (conv_w[:, :D_INNER], conv_w[:, D_INNER:D_INNER + BC_WIDTH],
                        conv_w[:, D_INNER + BC_WIDTH:])
    cb = conv_b.reshape(1, -1)
    cb_x, cb_b, cb_c = (cb[:, :D_INNER], cb[:, D_INNER:D_INNER + BC_WIDTH],
                        cb[:, D_INNER + BC_WIDTH:])
    dtb = jnp.pad(dt_bias, (0, pad)).reshape(1, LANES)
    alog = jnp.pad(a_log, (0, pad)).reshape(1, LANES)
    dskip = jnp.repeat(d_skip, SSM_HEAD_DIM).reshape(1, D_INNER)
    const = lambda bi, i: (0, 0)
    return pl.pallas_call(
        functools.partial(_ssd_kernel, chunk=chunk),
        grid=(b, s // chunk),
        in_specs=[
            pl.BlockSpec((None, chunk, D_INNER), lambda bi, i: (bi, i, PB_Z // D_INNER)),
            pl.BlockSpec((None, chunk, D_INNER), lambda bi, i: (bi, i, PB_XS // D_INNER)),
            pl.BlockSpec((None, chunk, BC_WIDTH), lambda bi, i: (bi, i, PB_BM // BC_WIDTH)),
            pl.BlockSpec((None, chunk, BC_WIDTH), lambda bi, i: (bi, i, PB_CM // BC_WIDTH)),
            pl.BlockSpec((None, chunk, LANES), lambda bi, i: (bi, i, PF_DT // LANES)),
            pl.BlockSpec((taps * chunk, chunk), const),
            pl.BlockSpec((CONV_WIDTH, D_INNER), const),
            pl.BlockSpec((CONV_WIDTH, BC_WIDTH), const),
            pl.BlockSpec((CONV_WIDTH, BC_WIDTH), const),
            pl.BlockSpec((1, D_INNER), const),
            pl.BlockSpec((1, BC_WIDTH), const),
            pl.BlockSpec((1, BC_WIDTH), const),
            pl.BlockSpec((1, LANES), const),
            pl.BlockSpec((1, LANES), const),
            pl.BlockSpec((1, D_INNER), const),
            pl.BlockSpec((1, D_INNER), const),
        ],
        out_specs=pl.BlockSpec((None, chunk, D_INNER), lambda bi, i: (bi, i, 0)),
        out_shape=jax.ShapeDtypeStruct((b, s, D_INNER), BF16),
        scratch_shapes=[
            pltpu.VMEM((SUBLANES, D_INNER), F32),
            pltpu.VMEM((SUBLANES, BC_WIDTH), F32),
            pltpu.VMEM((SUBLANES, BC_WIDTH), F32),
            pltpu.VMEM((D_STATE, D_INNER), F32),
            pltpu.VMEM((chunk, D_INNER), F32),
        ],
        compiler_params=_params("arbitrary", "arbitrary"),
        name="ssd_mixer",
    )(proj_b, proj_b, proj_b, proj_b, proj_f, shift, cw_x, cw_b, cw_c, cb_x, cb_b, cb_c,
      dtb, alog, dskip, ssm_norm_g.reshape(1, D_INNER))


def _mix_out_kernel(att_ref, y_ref, ga_ref, gs_ref, x_ref, gate_ref, g_ref,
                    woa_ref, wos_ref, wout_ref, o_ref):
    ba = _dot(att_ref[...], woa_ref[...])
    bs = _dot(y_ref[...], wos_ref[...])
    merged = (jax.nn.sigmoid(ga_ref[...].astype(F32)) * ba
              + jax.nn.sigmoid(gs_ref[...].astype(F32)) * bs)
    out = _dot(merged.astype(BF16), wout_ref[...])
    normed = out * _rms_scale(out) * g_ref[...]
    o_ref[...] = x_ref[...] + gate_ref[...] * normed


def _mix_out(att, y, proj_b, x, gate, g, woa, wos, wout):
    b, s, d = x.shape
    tm = min(OUT_TM, s)
    const = lambda bi, i: (0, 0)
    return pl.pallas_call(
        _mix_out_kernel,
        grid=(b, s // tm),
        in_specs=[
            pl.BlockSpec((None, tm, ATT_WIDTH), lambda bi, i: (bi, i, 0)),
            pl.BlockSpec((None, tm, D_INNER), lambda bi, i: (bi, i, 0)),
            pl.BlockSpec((None, tm, d), lambda bi, i: (bi, i, PB_GATT // d)),
            pl.BlockSpec((None, tm, d), lambda bi, i: (bi, i, PB_GSSD // d)),
            pl.BlockSpec((None, tm, d), lambda bi, i: (bi, i, 0)),
            pl.BlockSpec((None, 1, d), lambda bi, i: (bi, 0, 0)),
            pl.BlockSpec((1, d), const),
            pl.BlockSpec((ATT_WIDTH, d), const),
            pl.BlockSpec((D_INNER, d), const),
            pl.BlockSpec((d, d), const),
        ],
        out_specs=pl.BlockSpec((None, tm, d), lambda bi, i: (bi, i, 0)),
        out_shape=jax.ShapeDtypeStruct((b, s, d), F32),
        compiler_params=_params("arbitrary", "arbitrary"),
        name="mix_out",
    )(att, y, proj_b, proj_b, x, gate, g, woa, wos, wout)


def _mlp_kernel(x_ref, g_in_ref, sc_ref, sh_ref, wup_ref, wdn_ref, gate_ref,
                g_out_ref, o_ref, h_ref, acc_ref):
    j = pl.program_id(2)

    @pl.when(j == 0)
    def _():
        x = x_ref[...]
        h = x * _rms_scale(x) * g_in_ref[...]
        h = h * (1.0 + sc_ref[...]) + sh_ref[...]
        h_ref[...] = h.astype(BF16)
        acc_ref[...] = jnp.zeros_like(acc_ref)

    up = jnp.maximum(_dot(h_ref[...], wup_ref[...]), 0.0)
    acc_ref[...] += _dot((up * up).astype(BF16), wdn_ref[...])

    @pl.when(j == pl.num_programs(2) - 1)
    def _():
        y = acc_ref[...]
        normed = y * _rms_scale(y) * g_out_ref[...]
        o_ref[...] = x_ref[...] + gate_ref[...] * normed


def _mlp(x, g_in, sc, sh, wup, wdn, gate, g_out):
    b, s, d = x.shape
    ff = wup.shape[1]
    tm = min(MLP_TM, s)
    per_b = lambda bi, i, j: (bi, 0, 0)
    const = lambda bi, i, j: (0, 0)
    return pl.pallas_call(
        _mlp_kernel,
        grid=(b, s // tm, ff // MLP_TF),
        in_specs=[
            pl.BlockSpec((None, tm, d), lambda bi, i, j: (bi, i, 0)),
            pl.BlockSpec((1, d), const),
            pl.BlockSpec((None, 1, d), per_b),
            pl.BlockSpec((None, 1, d), per_b),
            pl.BlockSpec((d, MLP_TF), lambda bi, i, j: (0, j)),
            pl.BlockSpec((MLP_TF, d), lambda bi, i, j: (j, 0)),
            pl.BlockSpec((None, 1, d), per_b),
            pl.BlockSpec((1, d), const),
        ],
        out_specs=pl.BlockSpec((None, tm, d), lambda bi, i, j: (bi, i, 0)),
        out_shape=jax.ShapeDtypeStruct((b, s, d), F32),
        scratch_shapes=[pltpu.VMEM((tm, d), BF16), pltpu.VMEM((tm, d), F32)],
        compiler_params=_params("arbitrary", "arbitrary", "arbitrary"),
        name="mlp",
    )(x, g_in, sc, sh, wup, wdn, gate, g_out)


def _split_offsets():
    sizes = (ATT_WIDTH, KV_RANK, IDX_HEADS * IDX_DIM, IDX_DIM, IDX_HEADS,
             D_INNER, D_INNER, BC_WIDTH, BC_WIDTH, N_SSM_HEADS, ATT_WIDTH, ATT_WIDTH)
    names = ("q", "kv", "qidx", "kidx", "widx", "z", "xs", "bm", "cm", "dt",
             "gatt", "gssd")
    offs = np.concatenate([[0], np.cumsum(sizes)])
    return {n: (int(offs[i]), int(offs[i + 1])) for i, n in enumerate(names)}


def _pack_w_in(w_in):
    sl = _split_offsets()
    col = lambda n: w_in[..., sl[n][0]:sl[n][1]]
    wb = jnp.concatenate([col(n) for n in
                          ("z", "xs", "bm", "cm", "q", "gatt", "gssd", "qidx")],
                         axis=-1).astype(BF16)
    lead = w_in.shape[:-1]
    zeros = lambda n: jnp.zeros(lead + (n,), w_in.dtype)
    wf = jnp.concatenate([
        col("kv"), col("kidx"), col("widx"), zeros(LANES - IDX_DIM - IDX_HEADS),
        col("dt"), zeros(LANES - N_SSM_HEADS)], axis=-1).astype(BF16)
    return wb, wf


def kernel(x, c, ada_w, ada_b, norm_g, w_in, kv_norm_g, kidx_norm_g, w_uk, w_uv,
           conv_w, conv_b, dt_bias, a_log, d_skip, ssm_norm_g, w_o_att, w_o_ssd,
           w_out, w_up, w_down):
    depth = ada_w.shape[0]
    d = x.shape[-1]
    mod = _modulation(c, ada_w, ada_b)
    wb_all, wf_all = _pack_w_in(w_in)
    wuk_all = jnp.transpose(w_uk, (0, 2, 1, 3)).astype(BF16)
    wuvt_all = jnp.transpose(w_uv, (0, 2, 3, 1)).astype(BF16)
    woa_all = w_o_att.astype(BF16)
    wos_all = w_o_ssd.astype(BF16)
    wout_all = w_out.astype(BF16)
    wup_all = w_up.astype(BF16)
    wdn_all = w_down.astype(BF16)

    for layer in range(depth):
        sh_m, sc_m, gt_m, sh_f, sc_f, gt_f = (mod[layer, :, k] for k in range(N_MOD))
        ng = lambda k: norm_g[layer, k].reshape(1, d)
        proj_b, proj_f = _input_projection(x, ng(0), sc_m, sh_m,
                                           wb_all[layer], wf_all[layer])
        ckv, ckvt, kidx, wt = _latent_prep(
            proj_f, kv_norm_g[layer].reshape(1, KV_RANK),
            kidx_norm_g[layer].reshape(1, IDX_DIM))
        att = _dsa_attention(proj_b, wt, ckv, ckvt, kidx,
                             wuk_all[layer], wuvt_all[layer])
        y = _ssd(proj_b, proj_f, conv_w[layer], conv_b[layer], dt_bias[layer],
                 a_log[layer], d_skip[layer], ssm_norm_g[layer])
        x = _mix_out(att, y, proj_b, x, gt_m, ng(1), woa_all[layer],
                     wos_all[layer], wout_all[layer])
        x = _mlp(x, ng(2), sc_f, sh_f, wup_all[layer], wdn_all[layer], gt_f, ng(3))
    return x
```

```python
import functools

import jax
import jax.numpy as jnp
import numpy as np
from jax import lax
from jax.experimental import pallas as pl
from jax.experimental.pallas import tpu as pltpu

F32 = jnp.float32
BF16 = jnp.bfloat16
I32 = jnp.int32
I16 = jnp.int16

N_ATT_HEADS = 8
ATT_HEAD_DIM = 128
ATT_WIDTH = N_ATT_HEADS * ATT_HEAD_DIM
KV_RANK = 256
IDX_HEADS = 8
IDX_DIM = 64
TOPK_MAX = 256
D_STATE = 128
SSM_HEAD_DIM = 64
N_SSM_GROUPS = 4
HEADS_PER_GROUP = 8
N_SSM_HEADS = N_SSM_GROUPS * HEADS_PER_GROUP
D_INNER = N_SSM_HEADS * SSM_HEAD_DIM
BC_WIDTH = N_SSM_GROUPS * D_STATE
CONV_WIDTH = 4
N_MOD = 6
EPS = 1e-6

LANES = 128
SUBLANES = 8
VMEM_LIMIT_BYTES = 52 * 1024 * 1024

MXU_WIDTH = 256
Q_TILE = 256
KEY_CHUNK = 256
CKVT_ROWS = KV_RANK + 16
SSD_CHUNK = 128
CONV_BLOCK = 512
INPROJ_TM = 2048
INPROJ_TN = 512
OUT_TM = 512
MLP_TM = 1024
MLP_TF = 1024
MOD_TN = 1536

INT_MIN = -2147483648
I16_MIN = -32768
I16_MAX = 32767
HALF_BIAS = 32768
PACK_ROWS = 16
LOG2_E = 1.4426950408889634
COUNT_ACCS = 8
SELECT_GROUP = 5
NEG_BIG = -1e30

PB_Z, PB_XS, PB_Q, PB_GATT, PB_GSSD, PB_BM, PB_CM, PB_QIDX = (
    0, 2048, 4096, 5120, 6144, 7168, 7680, 8192)
PB_WIDTH = 8704
PF_KV, PF_KIDX, PF_DT = 0, 256, 384
PF_WIDTH = 512


def _params(*sem):
    return pltpu.CompilerParams(dimension_semantics=sem,
                                vmem_limit_bytes=VMEM_LIMIT_BYTES)


def _dot(a, b):
    return jnp.dot(a, b, preferred_element_type=F32)


def _dot_nt(a, b):
    return lax.dot_general(a, b, (((1,), (1,)), ((), ())),
                           preferred_element_type=F32)


def _silu(x):
    return x * jax.nn.sigmoid(x)


def _rms_scale(x):
    return lax.rsqrt(jnp.mean(x * x, axis=-1, keepdims=True) + EPS)


def _mod_kernel(c_ref, w_ref, b_ref, o_ref):
    c = c_ref[...]
    c_act = (c * jax.nn.sigmoid(c)).astype(BF16)
    o_ref[...] = _dot(c_act, w_ref[...].astype(BF16)) + b_ref[...]


def _modulation(c, ada_w, ada_b):
    n_layers, d, n = ada_w.shape
    b = c.shape[0]
    rows = ((b + SUBLANES - 1) // SUBLANES) * SUBLANES
    c_pad = jnp.zeros((rows, d), F32).at[:b].set(c)
    out = pl.pallas_call(
        _mod_kernel,
        grid=(n_layers, n // MOD_TN),
        in_specs=[
            pl.BlockSpec((rows, d), lambda l, j: (0, 0)),
            pl.BlockSpec((None, d, MOD_TN), lambda l, j: (l, 0, j)),
            pl.BlockSpec((None, 1, MOD_TN), lambda l, j: (l, 0, j)),
        ],
        out_specs=pl.BlockSpec((None, rows, MOD_TN), lambda l, j: (l, 0, j)),
        out_shape=jax.ShapeDtypeStruct((n_layers, rows, n), F32),
        compiler_params=_params("arbitrary", "arbitrary"),
        name="adaln_mod",
    )(c_pad, ada_w, ada_b.reshape(n_layers, 1, n))
    return out[:, :b].reshape(n_layers, b, N_MOD, 1, d)


def _inproj_kernel(x_ref, g_ref, sc_ref, sh_ref, wb_ref, wf_ref,
                   ob_ref, of_ref, h_ref, *, n_bf16_tiles):
    j = pl.program_id(2)

    @pl.when(j == 0)
    def _():
        x = x_ref[...]
        h = x * _rms_scale(x) * g_ref[...]
        h = h * (1.0 + sc_ref[...]) + sh_ref[...]
        h_ref[...] = h.astype(BF16)

    @pl.when(j < n_bf16_tiles)
    def _():
        ob_ref[...] = _dot(h_ref[...], wb_ref[...]).astype(BF16)

    @pl.when(j == n_bf16_tiles)
    def _():
        of_ref[...] = _dot(h_ref[...], wf_ref[...])


def _input_projection(x, g, sc, sh, wb, wf):
    b, s, d = x.shape
    tm = min(INPROJ_TM, s)
    nj = PB_WIDTH // INPROJ_TN
    last = nj - 1
    return pl.pallas_call(
        functools.partial(_inproj_kernel, n_bf16_tiles=nj),
        grid=(b, s // tm, nj + 1),
        in_specs=[
            pl.BlockSpec((None, tm, d), lambda bi, i, j: (bi, i, 0)),
            pl.BlockSpec((1, d), lambda bi, i, j: (0, 0)),
            pl.BlockSpec((None, 1, d), lambda bi, i, j: (bi, 0, 0)),
            pl.BlockSpec((None, 1, d), lambda bi, i, j: (bi, 0, 0)),
            pl.BlockSpec((d, INPROJ_TN),
                         lambda bi, i, j: (0, jnp.minimum(j, last))),
            pl.BlockSpec((d, PF_WIDTH), lambda bi, i, j: (0, 0)),
        ],
        out_specs=[
            pl.BlockSpec((None, tm, INPROJ_TN),
                         lambda bi, i, j: (bi, i, jnp.minimum(j, last))),
            pl.BlockSpec((None, tm, PF_WIDTH), lambda bi, i, j: (bi, i, 0)),
        ],
        out_shape=[
            jax.ShapeDtypeStruct((b, s, PB_WIDTH), BF16),
            jax.ShapeDtypeStruct((b, s, PF_WIDTH), F32),
        ],
        scratch_shapes=[pltpu.VMEM((tm, d), BF16)],
        compiler_params=_params("arbitrary", "arbitrary", "arbitrary"),
        name="norm_inproj",
    )(x, g, sc, sh, wb, wf)


def _prep_kernel(kv_ref, kw_ref, gkv_ref, gk_ref, ckv_ref, ckvt_ref,
                 kidx_ref, wt_ref, *, n_chunks):
    kv = kv_ref[...]
    ckv = kv * _rms_scale(kv) * gkv_ref[...]
    ckv_ref[...] = ckv.astype(BF16)
    extra_row = lax.broadcasted_iota(I32, (CKVT_ROWS - KV_RANK, KEY_CHUNK), 0)
    extra = jnp.where(extra_row == 0, 1.0, 0.0).astype(BF16)
    for c in range(n_chunks):
        blk = ckv[c * KEY_CHUNK:(c + 1) * KEY_CHUNK, :]
        ckvt_ref[c] = jnp.concatenate([blk.T.astype(BF16), extra], axis=0)
    kw = kw_ref[...]
    k = kw[:, :IDX_DIM]
    kn = k * _rms_scale(k) * gk_ref[...]
    kidx_ref[...] = kn.astype(BF16)
    kwt = kw.T
    wt_ref[...] = kwt[IDX_DIM:IDX_DIM + IDX_HEADS, :] * (IDX_HEADS ** -0.5)


def _latent_prep(proj_f, kv_norm_g, kidx_norm_g):
    b, s, _ = proj_f.shape
    ts = min(1024, s)
    n_chunks = ts // KEY_CHUNK
    return pl.pallas_call(
        functools.partial(_prep_kernel, n_chunks=n_chunks),
        grid=(b, s // ts),
        in_specs=[
            pl.BlockSpec((None, ts, KV_RANK),
                         lambda bi, i: (bi, i, PF_KV // KV_RANK)),
            pl.BlockSpec((None, ts, LANES),
                         lambda bi, i: (bi, i, PF_KIDX // LANES)),
            pl.BlockSpec((1, KV_RANK), lambda bi, i: (0, 0)),
            pl.BlockSpec((1, IDX_DIM), lambda bi, i: (0, 0)),
        ],
        out_specs=[
            pl.BlockSpec((None, ts, KV_RANK), lambda bi, i: (bi, i, 0)),
            pl.BlockSpec((None, n_chunks, CKVT_ROWS, KEY_CHUNK),
                         lambda bi, i: (bi, i, 0, 0)),
            pl.BlockSpec((None, ts, IDX_DIM), lambda bi, i: (bi, i, 0)),
            pl.BlockSpec((None, IDX_HEADS, ts), lambda bi, i: (bi, 0, i)),
        ],
        out_shape=[
            jax.ShapeDtypeStruct((b, s, KV_RANK), BF16),
            jax.ShapeDtypeStruct((b, s // KEY_CHUNK, CKVT_ROWS, KEY_CHUNK), BF16),
            jax.ShapeDtypeStruct((b, s, IDX_DIM), BF16),
            jax.ShapeDtypeStruct((b, IDX_HEADS, s), F32),
        ],
        compiler_params=_params("arbitrary", "arbitrary"),
        name="latent_prep",
    )(proj_f, proj_f, kv_norm_g, kidx_norm_g)


def _dsa_kernel(q_ref, qidx_ref, wt_ref, ckv_ref, ckvt_ref, kidx_ref,
                wuk_ref, wuvt_ref, o_ref,
                keys_ref, khi_ref, klo_ref, lom_ref, qall_ref, qabs_ref, acc_ref,
                tsel_ref, p_ref,
                *, topk, pos_bits):
    qb = pl.program_id(1)
    n_chunks = (qb * Q_TILE + Q_TILE + KEY_CHUNK - 1) // KEY_CHUNK
    q_pos = qb * Q_TILE + lax.broadcasted_iota(I32, (KEY_CHUNK, Q_TILE), 1)
    row_iota = lax.broadcasted_iota(I32, (KEY_CHUNK, Q_TILE), 0)

    for h in range(IDX_HEADS):
        qall_ref[h * Q_TILE:(h + 1) * Q_TILE, :] = (
            qidx_ref[:, h * IDX_DIM:(h + 1) * IDX_DIM])

    n_full = n_chunks // 2
    n_pairs = (n_chunks + 1) // 2
    pair_rows = 2 * KEY_CHUNK

    def score_chunk(c):
        start = pl.multiple_of(c * KEY_CHUNK, KEY_CHUNK)
        ks = kidx_ref[pl.ds(start, KEY_CHUNK), :]
        logits = _dot_nt(ks, qall_ref[...])
        score = jnp.zeros((KEY_CHUNK, Q_TILE), F32)
        for h in range(IDX_HEADS):
            lg = logits[:, h * Q_TILE:(h + 1) * Q_TILE]
            score = score + wt_ref[h:h + 1, :] * jnp.maximum(lg, 0.0)
        bits = pltpu.bitcast(score, I32)
        key = bits ^ ((bits >> 31) & 0x7FFFFFFF)
        key = jnp.where(start + row_iota > q_pos, INT_MIN, key)
        keys_ref[pl.ds(start, KEY_CHUNK), :] = key
        khi_ref[pl.ds(start, KEY_CHUNK), :] = (key >> 16).astype(I16)
        klo_ref[pl.ds(start, KEY_CHUNK), :] = ((key & 0xFFFF) - HALF_BIAS).astype(I16)

    def score_pair(i, carry):
        score_chunk(2 * i)
        score_chunk(2 * i + 1)
        return carry

    lax.fori_loop(0, n_full, score_pair, 0)

    @pl.when(n_chunks % 2 == 1)
    def _():
        score_chunk(n_chunks - 1)
        pad = pl.multiple_of(n_chunks * KEY_CHUNK, KEY_CHUNK)
        never = jnp.full((KEY_CHUNK, Q_TILE), I16_MIN, I16)
        khi_ref[pl.ds(pad, KEY_CHUNK), :] = never
        klo_ref[pl.ds(pad, KEY_CHUNK), :] = never

    zero = jnp.zeros((1, Q_TILE), I32)

    def count16(ref, t):
        t16 = jnp.broadcast_to(t, (PACK_ROWS, Q_TILE)).astype(I16)

        def body(c, accs):
            start = pl.multiple_of(c * pair_rows, pair_rows)
            blk = ref[pl.ds(start, pair_rows), :]
            accs = list(accs)
            for r in range(pair_rows // PACK_ROWS):
                hit = jnp.where(blk[r * PACK_ROWS:(r + 1) * PACK_ROWS] >= t16,
                                jnp.int16(1), jnp.int16(0))
                accs[r % COUNT_ACCS] = accs[r % COUNT_ACCS] + hit
            return tuple(accs)
        zeros16 = jnp.zeros((PACK_ROWS, Q_TILE), I16)
        accs = lax.fori_loop(0, n_pairs, body, (zeros16,) * COUNT_ACCS)
        total = accs[0].astype(I32)
        for k in range(1, COUNT_ACCS):
            total = total + accs[k].astype(I32)
        return jnp.sum(total, axis=0, keepdims=True)

    def refine16(ref, target, cand, cand_cnt, n_bits, live=None):
        def bit_step(i, carry):
            cand, cand_cnt = carry
            trial = cand | (jnp.int32(1) << (n_bits - 1 - i))
            cnt = count16(ref, trial)
            keep = cnt >= target
            return jnp.where(keep, trial, cand), jnp.where(keep, cnt, cand_cnt)

        if live is None:
            return lax.fori_loop(0, n_bits, bit_step, (cand, cand_cnt))

        def unsettled(cnt):
            return jnp.max(jnp.where(live & (cnt != target), 1, 0))

        def group_step(state):
            g, cand, cand_cnt, _ = state
            for j in range(SELECT_GROUP):
                cand, cand_cnt = bit_step(g * SELECT_GROUP + j, (cand, cand_cnt))
            return g + 1, cand, cand_cnt, unsettled(cand_cnt)

        _, cand, cand_cnt, _ = lax.while_loop(
            lambda st: (st[0] < n_bits // SELECT_GROUP) & (st[3] > 0),
            group_step, (jnp.int32(0), cand, cand_cnt, unsettled(cand_cnt)))
        return cand, cand_cnt

    def select16(ref, target, floor_cnt, live=None):
        cnt0 = count16(ref, zero)
        ok = cnt0 >= target
        cand = jnp.where(ok, zero, jnp.full((1, Q_TILE), I16_MIN, I32))
        return refine16(ref, target, cand, jnp.where(ok, cnt0, floor_cnt), 15, live)

    thr_hi, cnt_ge_hi = select16(khi_ref, topk, zero)
    cnt_above = count16(khi_ref, thr_hi + 1)
    real = thr_hi != I16_MIN

    thr_hi16 = jnp.broadcast_to(thr_hi, (PACK_ROWS, Q_TILE)).astype(I16)

    def mask_low(c, carry):
        start = pl.multiple_of(c * pair_rows, pair_rows)
        for r in range(pair_rows // PACK_ROWS):
            rows = pl.ds(start + r * PACK_ROWS, PACK_ROWS)
            lom_ref[rows, :] = jnp.where(khi_ref[rows, :] == thr_hi16,
                                         klo_ref[rows, :], jnp.int16(I16_MIN))
        return carry

    lax.fori_loop(0, n_pairs, mask_low, 0)
    thr_lo, cnt_lo = select16(lom_ref, topk - cnt_above, cnt_ge_hi - cnt_above,
                              live=real)

    thr = jnp.where(real, (thr_hi << 16) | (thr_lo + HALF_BIAS), INT_MIN)
    thr_cnt = cnt_above + cnt_lo

    tsel_ref[...] = jnp.where(real, jnp.int32(2 ** 30), jnp.int32(-1))
    n_tied = jnp.max(jnp.where(real & (thr_cnt > topk), 1, 0))

    @pl.when(n_tied > 0)
    def _():
        cnt_gt = cnt_above + jnp.where(
            thr_lo < I16_MAX, count16(lom_ref, thr_lo + 1), 0)
        need = topk - cnt_gt
        thr_lo16 = jnp.broadcast_to(thr_lo, (PACK_ROWS, Q_TILE)).astype(I16)
        pack_iota = lax.broadcasted_iota(I32, (PACK_ROWS, Q_TILE), 0)

        def reversed_positions(c, carry):
            start = pl.multiple_of(c * pair_rows, pair_rows)
            for r in range(pair_rows // PACK_ROWS):
                rows = pl.ds(start + r * PACK_ROWS, PACK_ROWS)
                tied = ((khi_ref[rows, :] == thr_hi16)
                        & (lom_ref[rows, :] == thr_lo16))
                rev = (I16_MAX - (start + r * PACK_ROWS) - pack_iota).astype(I16)
                klo_ref[rows, :] = jnp.where(tied, rev, jnp.int16(I16_MIN))
            return carry

        lax.fori_loop(0, n_pairs, reversed_positions, 0)
        base = jnp.full((1, Q_TILE), I16_MAX + 1 - 2 ** pos_bits, I32)
        last_rev, _ = refine16(klo_ref, need, base, zero, pos_bits)
        tsel_ref[...] = jnp.where(real, I16_MAX - last_rev, jnp.int32(-1))

    tsel = tsel_ref[...]

    scale = ATT_HEAD_DIM ** -0.5 * LOG2_E
    for h in range(N_ATT_HEADS):
        qh = q_ref[:, h * ATT_HEAD_DIM:(h + 1) * ATT_HEAD_DIM]
        qa = _dot_nt(wuk_ref[h], qh) * scale
        qabs_ref[:, h * Q_TILE:(h + 1) * Q_TILE] = qa.astype(BF16)

    width = N_ATT_HEADS * Q_TILE
    acc_ref[...] = jnp.zeros_like(acc_ref)

    def probabilities(c, slot, m):
        start = pl.multiple_of(c * KEY_CHUNK, KEY_CHUNK)
        kv = ckv_ref[pl.ds(start, KEY_CHUNK), :]
        kk = keys_ref[pl.ds(start, KEY_CHUNK), :]
        pos = start + row_iota
        bias = jnp.where(
            kk > thr, 0.0,
            jnp.where(kk == thr, jnp.where(pos <= tsel, 0.0, NEG_BIG), NEG_BIG))
        m_out, alphas = [], []
        for grp in range(width // MXU_WIDTH):
            lo = grp * MXU_WIDTH
            st = _dot(kv, qabs_ref[:, lo:lo + MXU_WIDTH])
            for hh in range(MXU_WIDTH // Q_TILE):
                a, b = lo + hh * Q_TILE, lo + (hh + 1) * Q_TILE
                s_h = st[:, hh * Q_TILE:(hh + 1) * Q_TILE] + bias
                m_old = m[:, a:b]
                m_new = jnp.maximum(m_old, jnp.max(s_h, axis=0, keepdims=True))
                alpha = jnp.exp2(m_old - m_new)
                p = jnp.exp2(s_h - m_new)
                m_out.append(m_new)
                alphas.append(alpha)
                p_ref[slot, :, a:b] = p.astype(BF16)
        cat = lambda xs: jnp.concatenate(xs, axis=1)
        return cat(m_out), cat(alphas)

    def accumulate(c, slot, alpha):
        kvt = ckvt_ref[c]
        for grp in range(width // MXU_WIDTH):
            lo, hi = grp * MXU_WIDTH, (grp + 1) * MXU_WIDTH
            pv = _dot(kvt, p_ref[slot, :, lo:hi])
            acc_ref[:, lo:hi] = acc_ref[:, lo:hi] * alpha[:, lo:hi] + pv

    p_ref[1] = jnp.zeros((KEY_CHUNK, width), BF16)

    def attn_pair(i, carry):
        m, alpha_prev = carry
        c0 = 2 * i
        m, alpha0 = probabilities(c0, 0, m)
        accumulate(jnp.maximum(c0 - 1, 0), 1, alpha_prev)
        m, alpha1 = probabilities(c0 + 1, 1, m)
        accumulate(c0, 0, alpha0)
        return m, alpha1

    m0 = jnp.full((1, width), NEG_BIG, F32)
    ones = jnp.ones((1, width), F32)
    m_last, alpha_last = lax.fori_loop(0, n_full, attn_pair, (m0, ones))

    @pl.when(n_chunks % 2 == 0)
    def _():
        accumulate(n_chunks - 1, 1, alpha_last)

    @pl.when(n_chunks % 2 == 1)
    def _():
        last = n_chunks - 1
        _, alpha0 = probabilities(last, 0, m_last)
        accumulate(jnp.maximum(last - 1, 0), 1, alpha_last)
        accumulate(last, 0, alpha0)

    denom = acc_ref[KV_RANK:KV_RANK + 1, :]
    o_lat = (acc_ref[0:KV_RANK, :] * (1.0 / denom)).astype(BF16)
    for h in range(N_ATT_HEADS):
        ot = _dot(wuvt_ref[h], o_lat[:, h * Q_TILE:(h + 1) * Q_TILE])
        o_ref[:, h * ATT_HEAD_DIM:(h + 1) * ATT_HEAD_DIM] = ot.T.astype(BF16)


def _dsa_attention(proj_b, wt, ckv, ckvt, kidx, wuk, wuvt):
    b, s, _ = proj_b.shape
    topk = min(TOPK_MAX, s // 4)
    assert s % (2 * KEY_CHUNK) == 0 and s <= I16_MAX + 1
    width = N_ATT_HEADS * Q_TILE
    return pl.pallas_call(
        functools.partial(_dsa_kernel, topk=topk, pos_bits=(s - 1).bit_length()),
        grid=(b, s // Q_TILE),
        in_specs=[
            pl.BlockSpec((None, Q_TILE, ATT_WIDTH),
                         lambda bi, i: (bi, i, PB_Q // ATT_WIDTH)),
            pl.BlockSpec((None, Q_TILE, IDX_HEADS * IDX_DIM),
                         lambda bi, i: (bi, i, PB_QIDX // (IDX_HEADS * IDX_DIM))),
            pl.BlockSpec((None, IDX_HEADS, Q_TILE), lambda bi, i: (bi, 0, i)),
            pl.BlockSpec((None, s, KV_RANK), lambda bi, i: (bi, 0, 0)),
            pl.BlockSpec((None, s // KEY_CHUNK, CKVT_ROWS, KEY_CHUNK),
                         lambda bi, i: (bi, 0, 0, 0)),
            pl.BlockSpec((None, s, IDX_DIM), lambda bi, i: (bi, 0, 0)),
            pl.BlockSpec((N_ATT_HEADS, KV_RANK, ATT_HEAD_DIM),
                         lambda bi, i: (0, 0, 0)),
            pl.BlockSpec((N_ATT_HEADS, ATT_HEAD_DIM, KV_RANK),
                         lambda bi, i: (0, 0, 0)),
        ],
        out_specs=pl.BlockSpec((None, Q_TILE, ATT_WIDTH), lambda bi, i: (bi, i, 0)),
        out_shape=jax.ShapeDtypeStruct((b, s, ATT_WIDTH), BF16),
        scratch_shapes=[
            pltpu.VMEM((s, Q_TILE), I32),
            pltpu.VMEM((s, Q_TILE), I16),
            pltpu.VMEM((s, Q_TILE), I16),
            pltpu.VMEM((s, Q_TILE), I16),
            pltpu.VMEM((IDX_HEADS * Q_TILE, IDX_DIM), BF16),
            pltpu.VMEM((KV_RANK, width), BF16),
            pltpu.VMEM((CKVT_ROWS, width), F32),
            pltpu.VMEM((1, Q_TILE), I32),
            pltpu.VMEM((2, KEY_CHUNK, width), BF16),
        ],
        compiler_params=_params("arbitrary", "arbitrary"),
        name="dsa_attention",
    )(proj_b, proj_b, wt, ckv, ckvt, kidx, wuk, wuvt)


def _ssd_kernel(z_ref, xs_ref, bm_ref, cm_ref, dt_ref, shift_ref,
                cwx_ref, cwb_ref, cwc_ref, cbx_ref, cbb_ref, cbc_ref,
                dtb_ref, alog_ref, dskip_ref, ng_ref, y_ref,
                tx_ref, tb_ref, tc_ref, state_ref, yacc_ref, xc_ref, bc_ref, cc_ref,
                *, chunk):
    ci = pl.program_id(1)
    taps = CONV_WIDTH - 1

    @pl.when(ci == 0)
    def _():
        tx_ref[...] = jnp.zeros_like(tx_ref)
        tb_ref[...] = jnp.zeros_like(tb_ref)
        tc_ref[...] = jnp.zeros_like(tc_ref)
        state_ref[...] = jnp.zeros_like(state_ref)

    def conv_silu(in_ref, tail_ref, w_ref, b_ref, out_ref):
        for c0 in range(0, in_ref.shape[-1], CONV_BLOCK):
            cols = slice(c0, c0 + CONV_BLOCK)
            cur = in_ref[:, cols]
            shifted = _dot(shift_ref[...], cur)
            cur_f = cur.astype(F32)
            acc = b_ref[:, cols] + w_ref[taps:taps + 1, cols] * cur_f
            for k in range(taps):
                acc = acc + w_ref[k:k + 1, cols] * shifted[k * chunk:(k + 1) * chunk]
            tail = tail_ref[:, cols]
            row = lax.broadcasted_iota(I32, tail.shape, 0)
            fix = jnp.zeros_like(tail)
            for k in range(taps):
                back = taps - k
                fix = fix + jnp.where(
                    row < back, w_ref[k:k + 1, cols] * pltpu.roll(tail, back, 0), 0.0)
            acc = jnp.concatenate([acc[:SUBLANES] + fix, acc[SUBLANES:]], axis=0)
            tail_ref[:, cols] = cur_f[chunk - SUBLANES:]
            out_ref[:, cols] = _silu(acc)

    conv_silu(xs_ref, tx_ref, cwx_ref, cbx_ref, xc_ref)
    conv_silu(bm_ref, tb_ref, cwb_ref, cbb_ref, bc_ref)
    conv_silu(cm_ref, tc_ref, cwc_ref, cbc_ref, cc_ref)

    dt = jax.nn.softplus(dt_ref[...] + dtb_ref[...])
    a2 = -jnp.exp(alog_ref[...]) * LOG2_E
    r_io = lax.broadcasted_iota(I32, (chunk, chunk), 0)
    c_io = lax.broadcasted_iota(I32, (chunk, chunk), 1)
    tril = r_io >= c_io
    a_cum = jnp.dot(tril.astype(F32), dt * a2, precision=lax.Precision.HIGHEST,
                    preferred_element_type=F32)
    total = a_cum[chunk - 1:chunk, :]
    to_end_t = (jnp.exp2(total - a_cum) * dt).T
    src_t = (a_cum - jnp.log2(dt)).T
    first = lax.broadcasted_iota(I32, (1, LANES), 1) < SSM_HEAD_DIM

    for g in range(N_SSM_GROUPS):
        bg = bc_ref[:, g * D_STATE:(g + 1) * D_STATE]
        cg = cc_ref[:, g * D_STATE:(g + 1) * D_STATE]
        cb = _dot_nt(cg.astype(BF16), bg.astype(BF16))
        bg_t = bg.T
        for pair in range(HEADS_PER_GROUP // 2):
            h0 = g * HEADS_PER_GROUP + 2 * pair
            lo = h0 * SSM_HEAD_DIM
            lhs, bgw, e_last = [], [], []
            for h in (h0, h0 + 1):
                col = jnp.broadcast_to(a_cum[:, h:h + 1], (chunk, chunk))
                decay = jnp.exp2(jnp.where(tril, col - src_t[h:h + 1, :], -jnp.inf))
                ecol = jnp.exp2(col)
                lhs.append(jnp.concatenate(
                    [(cb * decay).astype(BF16), (cg * ecol).astype(BF16)], axis=1))
                bgw.append((bg_t * to_end_t[h:h + 1, :]).astype(BF16))
                e_last.append(ecol[chunk - 1:chunk, :])
            xs_pair = xc_ref[:, lo:lo + LANES].astype(BF16)
            prev = state_ref[:, lo:lo + LANES]
            rhs = jnp.concatenate([xs_pair, prev.astype(BF16)], axis=0)
            r = _dot(jnp.concatenate(lhs, axis=0), rhs)
            yacc_ref[:, lo:lo + LANES] = jnp.where(first, r[:chunk], r[chunk:])
            nw = _dot(jnp.concatenate(bgw, axis=0), xs_pair)
            keep = jnp.where(first, e_last[0], e_last[1])
            state_ref[:, lo:lo + LANES] = prev * keep + jnp.where(
                first, nw[:D_STATE], nw[D_STATE:])

    gw = D_INNER // N_SSM_GROUPS
    for g in range(N_SSM_GROUPS):
        cols = slice(g * gw, (g + 1) * gw)
        yg = yacc_ref[:, cols] + xc_ref[:, cols] * dskip_ref[:, cols]
        yg = yg * _silu(z_ref[:, cols].astype(F32))
        yn = yg * _rms_scale(yg) * ng_ref[:, cols]
        y_ref[:, cols] = yn.astype(BF16)


def _ssd(proj_b, proj_f, conv_w, conv_b, dt_bias, a_log, d_skip, ssm_norm_g):
    b, s, _ = proj_b.shape
    chunk = SSD_CHUNK
    assert chunk == D_STATE == LANES and s % chunk == 0
    pad = LANES - N_SSM_HEADS
    taps = CONV_WIDTH - 1
    t_io = np.arange(chunk)
    shift = np.concatenate(
        [(t_io[:, None] - (taps - k) == t_io[None, :]) for k in range(taps)], axis=0)
    shift = jnp.asarray(shift, BF16)
    cw_x, cw_b, cw_c = (conv_w[:, :D_INNER], conv_w[:, D_INNER:D_INNER + BC_WIDTH],
                        conv_w[:, D_INNER + BC_WIDTH:])
    cb = conv_b.reshape(1, -1)
    cb_x, cb_b, cb_c = (cb[:, :D_INNER], cb[:, D_INNER:D_INNER + BC_WIDTH],
                        cb[:, D_INNER + BC_WIDTH:])
    dtb = jnp.pad(dt_bias, (0, pad)).reshape(1, LANES)
    alog = jnp.pad(a_log, (0, pad)).reshape(1, LANES)
    dskip = jnp.repeat(d_skip, SSM_HEAD_DIM).reshape(1, D_INNER)
    const = lambda bi, i: (0, 0)
    return pl.pallas_call(
        functools.partial(_ssd_kernel, chunk=chunk),
        grid=(b, s // chunk),
        in_specs=[
            pl.BlockSpec((None, chunk, D_INNER), lambda bi, i: (bi, i, PB_Z // D_INNER)),
            pl.BlockSpec((None, chunk, D_INNER), lambda bi, i: (bi, i, PB_XS // D_INNER)),
            pl.BlockSpec((None, chunk, BC_WIDTH), lambda bi, i: (bi, i, PB_BM // BC_WIDTH)),
            pl.BlockSpec((None, chunk, BC_WIDTH), lambda bi, i: (bi, i, PB_CM // BC_WIDTH)),
            pl.BlockSpec((None, chunk, LANES), lambda bi, i: (bi, i, PF_DT // LANES)),
            pl.BlockSpec((taps * chunk, chunk), const),
            pl.BlockSpec((CONV_WIDTH, D_INNER), const),
            pl.BlockSpec((CONV_WIDTH, BC_WIDTH), const),
            pl.BlockSpec((CONV_WIDTH, BC_WIDTH), const),
            pl.BlockSpec((1, D_INNER), const),
            pl.BlockSpec((1, BC_WIDTH), const),
            pl.BlockSpec((1, BC_WIDTH), const),
            pl.BlockSpec((1, LANES), const),
            pl.BlockSpec((1, LANES), const),
            pl.BlockSpec((1, D_INNER), const),
            pl.BlockSpec((1, D_INNER), const),
        ],
        out_specs=pl.BlockSpec((None, chunk, D_INNER), lambda bi, i: (bi, i, 0)),
        out_shape=jax.ShapeDtypeStruct((b, s, D_INNER), BF16),
        scratch_shapes=[
            pltpu.VMEM((SUBLANES, D_INNER), F32),
            pltpu.VMEM((SUBLANES, BC_WIDTH), F32),
            pltpu.VMEM((SUBLANES, BC_WIDTH), F32),
            pltpu.VMEM((D_STATE, D_INNER), F32),
            pltpu.VMEM((chunk, D_INNER), F32),
            pltpu.VMEM((chunk, D_INNER), F32),
            pltpu.VMEM((chunk, BC_WIDTH), F32),
            pltpu.VMEM((chunk, BC_WIDTH), F32),
        ],
        compiler_params=_params("arbitrary", "arbitrary"),
        name="ssd_mixer",
    )(proj_b, proj_b, proj_b, proj_b, proj_f, shift, cw_x, cw_b, cw_c, cb_x, cb_b, cb_c,
      dtb, alog, dskip, ssm_norm_g.reshape(1, D_INNER))


def _mix_out_kernel(att_ref, y_ref, ga_ref, gs_ref, x_ref, gate_ref, g_ref,
                    woa_ref, wos_ref, wout_ref, o_ref):
    ba = _dot(att_ref[...], woa_ref[...])
    bs = _dot(y_ref[...], wos_ref[...])
    merged = (jax.nn.sigmoid(ga_ref[...].astype(F32)) * ba
              + jax.nn.sigmoid(gs_ref[...].astype(F32)) * bs)
    out = _dot(merged.astype(BF16), wout_ref[...])
    normed = out * _rms_scale(out) * g_ref[...]
    o_ref[...] = x_ref[...] + gate_ref[...] * normed


def _mix_out(att, y, proj_b, x, gate, g, woa, wos, wout):
    b, s, d = x.shape
    tm = min(OUT_TM, s)
    const = lambda bi, i: (0, 0)
    return pl.pallas_call(
        _mix_out_kernel,
        grid=(b, s // tm),
        in_specs=[
            pl.BlockSpec((None, tm, ATT_WIDTH), lambda bi, i: (bi, i, 0)),
            pl.BlockSpec((None, tm, D_INNER), lambda bi, i: (bi, i, 0)),
            pl.BlockSpec((None, tm, d), lambda bi, i: (bi, i, PB_GATT // d)),
            pl.BlockSpec((None, tm, d), lambda bi, i: (bi, i, PB_GSSD // d)),
            pl.BlockSpec((None, tm, d), lambda bi, i: (bi, i, 0)),
            pl.BlockSpec((None, 1, d), lambda bi, i: (bi, 0, 0)),
            pl.BlockSpec((1, d), const),
            pl.BlockSpec((ATT_WIDTH, d), const),
            pl.BlockSpec((D_INNER, d), const),
            pl.BlockSpec((d, d), const),
        ],
        out_specs=pl.BlockSpec((None, tm, d), lambda bi, i: (bi, i, 0)),
        out_shape=jax.ShapeDtypeStruct((b, s, d), F32),
        compiler_params=_params("arbitrary", "arbitrary"),
        name="mix_out",
    )(att, y, proj_b, proj_b, x, gate, g, woa, wos, wout)


def _mlp_kernel(x_ref, g_in_ref, sc_ref, sh_ref, wup_ref, wdn_ref, gate_ref,
                g_out_ref, o_ref, h_ref, acc_ref):
    j = pl.program_id(2)

    @pl.when(j == 0)
    def _():
        x = x_ref[...]
        h = x * _rms_scale(x) * g_in_ref[...]
        h = h * (1.0 + sc_ref[...]) + sh_ref[...]
        h_ref[...] = h.astype(BF16)
        acc_ref[...] = jnp.zeros_like(acc_ref)

    up = jnp.maximum(_dot(h_ref[...], wup_ref[...]), 0.0)
    acc_ref[...] += _dot((up * up).astype(BF16), wdn_ref[...])

    @pl.when(j == pl.num_programs(2) - 1)
    def _():
        y = acc_ref[...]
        normed = y * _rms_scale(y) * g_out_ref[...]
        o_ref[...] = x_ref[...] + gate_ref[...] * normed


def _mlp(x, g_in, sc, sh, wup, wdn, gate, g_out):
    b, s, d = x.shape
    ff = wup.shape[1]
    tm = min(MLP_TM, s)
    per_b = lambda bi, i, j: (bi, 0, 0)
    const = lambda bi, i, j: (0, 0)
    return pl.pallas_call(
        _mlp_kernel,
        grid=(b, s // tm, ff // MLP_TF),
        in_specs=[
            pl.BlockSpec((None, tm, d), lambda bi, i, j: (bi, i, 0)),
            pl.BlockSpec((1, d), const),
            pl.BlockSpec((None, 1, d), per_b),
            pl.BlockSpec((None, 1, d), per_b),
            pl.BlockSpec((d, MLP_TF), lambda bi, i, j: (0, j)),
            pl.BlockSpec((MLP_TF, d), lambda bi, i, j: (j, 0)),
            pl.BlockSpec((None, 1, d), per_b),
            pl.BlockSpec((1, d), const),
        ],
        out_specs=pl.BlockSpec((None, tm, d), lambda bi, i, j: (bi, i, 0)),
        out_shape=jax.ShapeDtypeStruct((b, s, d), F32),
        scratch_shapes=[pltpu.VMEM((tm, d), BF16), pltpu.VMEM((tm, d), F32)],
        compiler_params=_params("arbitrary", "arbitrary", "arbitrary"),
        name="mlp",
    )(x, g_in, sc, sh, wup, wdn, gate, g_out)


def _split_offsets():
    sizes = (ATT_WIDTH, KV_RANK, IDX_HEADS * IDX_DIM, IDX_DIM, IDX_HEADS,
             D_INNER, D_INNER, BC_WIDTH, BC_WIDTH, N_SSM_HEADS, ATT_WIDTH, ATT_WIDTH)
    names = ("q", "kv", "qidx", "kidx", "widx", "z", "xs", "bm", "cm", "dt",
             "gatt", "gssd")
    offs = np.concatenate([[0], np.cumsum(sizes)])
    return {n: (int(offs[i]), int(offs[i + 1])) for i, n in enumerate(names)}


def _pack_w_in(w_in):
    sl = _split_offsets()
    col = lambda n: w_in[..., sl[n][0]:sl[n][1]]
    wb = jnp.concatenate([col(n) for n in
                          ("z", "xs", "q", "gatt", "gssd", "bm", "cm", "qidx")],
                         axis=-1).astype(BF16)
    lead = w_in.shape[:-1]
    zeros = lambda n: jnp.zeros(lead + (n,), w_in.dtype)
    wf = jnp.concatenate([
        col("kv"), col("kidx"), col("widx"), zeros(LANES - IDX_DIM - IDX_HEADS),
        col("dt"), zeros(LANES - N_SSM_HEADS)], axis=-1).astype(BF16)
    return wb, wf


def kernel(x, c, ada_w, ada_b, norm_g, w_in, kv_norm_g, kidx_norm_g, w_uk, w_uv,
           conv_w, conv_b, dt_bias, a_log, d_skip, ssm_norm_g, w_o_att, w_o_ssd,
           w_out, w_up, w_down):
    depth = ada_w.shape[0]
    d = x.shape[-1]
    mod = _modulation(c, ada_w, ada_b)
    wb_all, wf_all = _pack_w_in(w_in)
    wuk_all = jnp.transpose(w_uk, (0, 2, 1, 3)).astype(BF16)
    wuvt_all = jnp.transpose(w_uv, (0, 2, 3, 1)).astype(BF16)
    woa_all = w_o_att.astype(BF16)
    wos_all = w_o_ssd.astype(BF16)
    wout_all = w_out.astype(BF16)
    wup_all = w_up.astype(BF16)
    wdn_all = w_down.astype(BF16)

    for layer in range(depth):
        sh_m, sc_m, gt_m, sh_f, sc_f, gt_f = (mod[layer, :, k] for k in range(N_MOD))
        ng = lambda k: norm_g[layer, k].reshape(1, d)
        proj_b, proj_f = _input_projection(x, ng(0), sc_m, sh_m,
                                           wb_all[layer], wf_all[layer])
        ckv, ckvt, kidx, wt = _latent_prep(
            proj_f, kv_norm_g[layer].reshape(1, KV_RANK),
            kidx_norm_g[layer].reshape(1, IDX_DIM))
        att = _dsa_attention(proj_b, wt, ckv, ckvt, kidx,
                             wuk_all[layer], wuvt_all[layer])
        y = _ssd(proj_b, proj_f, conv_w[layer], conv_b[layer], dt_bias[layer],
                 a_log[layer], d_skip[layer], ssm_norm_g[layer])
        x = _mix_out(att, y, proj_b, x, gt_m, ng(1), woa_all[layer],
                     wos_all[layer], wout_all[layer])
        x = _mlp(x, ng(2), sc_f, sh_f, wup_all[layer], wdn_all[layer], gt_f, ng(3))
    return x
```
